```python
import math
import jax, jax.numpy as jnp
from jax import lax
import numpy as np

D_MODEL = 1024
BATCH = 8
SEQ = 4096
DEPTH = 2

GRID_W = 64
CTX_LEN = 256
NORM_EPS = 1e-6
FFN_DIM = 4 * D_MODEL

HEAD_DIM = 64
RWKV_DIM = D_MODEL // 2
RWKV_HEADS = RWKV_DIM // HEAD_DIM
DECAY_LORA = 64
AAA_LORA = 64
GATE_LORA = 128
RWKV_COLS = 3 * RWKV_DIM + DECAY_LORA + AAA_LORA + GATE_LORA
RWKV_GN_EPS = 64e-5
DIFF_V = 2 * HEAD_DIM
DIFF_HEADS = (D_MODEL // 2) // DIFF_V
DIFF_QK = DIFF_HEADS * 2 * HEAD_DIM
DIFF_COLS = 2 * DIFF_QK + DIFF_HEADS * DIFF_V
HYB_IN = RWKV_COLS + DIFF_COLS
MIX_WIDTH = RWKV_DIM + DIFF_HEADS * DIFF_V
Q_BLOCK = 128
ROPE_BASE = 10000.0
ROPE_AXIS = HEAD_DIM // 2

D_INNER = 2 * D_MODEL
SSD_HEAD_DIM = 64
SSD_HEADS = D_INNER // SSD_HEAD_DIM
SSD_GROUPS = 8
SSD_HPG = SSD_HEADS // SSD_GROUPS
D_STATE = 128
CONV_W = 5
CONV_DIM = D_INNER + 2 * SSD_GROUPS * D_STATE
SSD_IN = D_INNER + CONV_DIM + SSD_HEADS
CHUNK = 128

N_EVEN = (DEPTH + 1) // 2
N_ODD = DEPTH // 2

kernel_name = "hybrid_rwkv7_diffattn_ssd_dit"


def rmsnorm(x, g, eps=NORM_EPS):
    xf = x.astype(jnp.float32)
    y = xf * lax.rsqrt(jnp.mean(xf * xf, axis=-1, keepdims=True) + eps)
    return (y * g.astype(jnp.float32)).astype(x.dtype)


def centred_shift(p):
    prev = jnp.pad(p[:, :-1], ((0, 0), (1, 0), (0, 0)))
    nxt = jnp.pad(p[:, 1:], ((0, 0), (0, 1), (0, 0)))
    return 0.5 * (prev + nxt)


def dwconv_centred(x, w, b):
    k = w.shape[0]
    out = lax.conv_general_dilated(x, w[:, None, :].astype(x.dtype), window_strides=(1,),
                                   padding=[(k // 2, k // 2)],
                                   dimension_numbers=('NWC', 'WIO', 'NWC'),
                                   feature_group_count=x.shape[-1])
    return out + b


def axial_angles(t_len):
    n_rows = t_len // GRID_W
    rows = jnp.broadcast_to(jnp.arange(n_rows)[:, None], (n_rows, GRID_W)).reshape(-1)
    cols = jnp.broadcast_to(jnp.arange(GRID_W)[None, :], (n_rows, GRID_W)).reshape(-1)
    inv = ROPE_BASE ** (-jnp.arange(0, ROPE_AXIS, 2, dtype=jnp.float32) / ROPE_AXIS)
    return rows.astype(jnp.float32)[:, None] * inv, cols.astype(jnp.float32)[:, None] * inv


def rope_half(x, ang):
    h = ROPE_AXIS // 2
    x1, x2 = x[..., :h], x[..., h:]
    cos, sin = jnp.cos(ang).astype(x.dtype), jnp.sin(ang).astype(x.dtype)
    return jnp.concatenate([x1 * cos - x2 * sin, x2 * cos + x1 * sin], axis=-1)


def rope_2d(x, ang_r, ang_c):
    ar = ang_r[:, None, None, :]
    ac = ang_c[:, None, None, :]
    return jnp.concatenate([rope_half(x[..., :ROPE_AXIS], ar), rope_half(x[..., ROPE_AXIS:], ac)], axis=-1)


def rwkv_prepare(p, mu, w0, w_up, a0, a_up, g_up, k_k, k_a):
    bsz, t_len = p.shape[:2]
    p = p + (centred_shift(p) - p) * mu
    r, k, v, wd, ad, gd = jnp.split(p, [RWKV_DIM, 2 * RWKV_DIM, 3 * RWKV_DIM,
                                        3 * RWKV_DIM + DECAY_LORA,
                                        3 * RWKV_DIM + DECAY_LORA + AAA_LORA], axis=-1)
    hd = lambda t: t.reshape(bsz, t_len, RWKV_HEADS, HEAD_DIM)
    kkf = hd(k * k_k).astype(jnp.float32)
    kk = kkf * lax.rsqrt(jnp.sum(kkf * kkf, axis=-1, keepdims=True) + 1e-12)
    g = jax.nn.sigmoid(gd) @ g_up
    dirs = []
    for d in range(2):
        wlog = -jax.nn.softplus(-(w0[d] + jnp.tanh(wd) @ w_up[d])) - 0.5
        decay = jnp.exp(-jnp.exp(wlog.astype(jnp.float32)))
        a = jax.nn.sigmoid(a0[d] + ad @ a_up[d])
        kmod = k * (1.0 + (a - 1.0) * k_a)
        dirs.append((hd(decay), hd(kmod), kk * hd(a).astype(jnp.float32)))
    return hd(r), hd(v), kk, g, dirs


def wkv_scan(r, decay, k, v, kk, b, s0, reverse):
    xs = tuple(jnp.moveaxis(t.astype(jnp.float32), 1, 0) for t in (r, decay, k, v, kk, b))

    def step(s, inp):
        r_t, w_t, k_t, v_t, kk_t, b_t = inp
        sa = jnp.einsum('bhvk,bhk->bhv', s, -kk_t)
        s = s * w_t[:, :, None, :] + sa[..., None] * b_t[:, :, None, :] + v_t[..., None] * k_t[:, :, None, :]
        return s, jnp.einsum('bhvk,bhk->bhv', s, r_t)

    s, ys = lax.scan(step, s0, xs, reverse=reverse)
    return jnp.moveaxis(ys, 0, 1), s


def wkv_bidir(prep, s0s):
    r, v, kk, _, dirs = prep
    (df, kf, bf), (db, kb, bb) = dirs
    yf, sf = wkv_scan(r, df, kf, v, kk, bf, s0s[0], False)
    yb, sb = wkv_scan(r, db, kb, v, kk, bb, s0s[1], True)
    return yf + yb, sf, sb


def rwkv_output(y, prep, r_k, ln_w, ln_b):
    r, v, _, g, dirs = prep
    bsz, t_len = r.shape[:2]
    mean = jnp.mean(y, axis=-1, keepdims=True)
    var = jnp.mean(jnp.square(y - mean), axis=-1, keepdims=True)
    yn = ((y - mean) * lax.rsqrt(var + RWKV_GN_EPS)).reshape(bsz, t_len, RWKV_DIM)
    yn = (yn * ln_w + ln_b).astype(r.dtype)
    k_bonus = 0.5 * (dirs[0][1] + dirs[1][1])
    bonus = (jnp.sum(r * k_bonus * r_k, axis=-1, keepdims=True) * v).reshape(bsz, t_len, RWKV_DIM)
    return (yn + bonus) * g


def diff_combine(s, lam, v):
    p = jax.nn.softmax(s.astype(jnp.float32), axis=-1)
    w = p[:, :, 0] - lam * p[:, :, 1]
    return jnp.einsum('bhqk,bhkd->bhqd', w.astype(v.dtype), v)


def diff_attn_latent(q_rot, q_plain, k_rot, k_ctx, v_all, lam):
    bsz, t_len = q_rot.shape[:2]
    nblk = t_len // Q_BLOCK
    to_blocks = lambda q: q.reshape(bsz, nblk, Q_BLOCK, DIFF_HEADS, 2, HEAD_DIM).transpose(1, 0, 3, 4, 2, 5)
    kr = k_rot.transpose(0, 2, 3, 1, 4)
    kc = k_ctx.transpose(0, 2, 3, 1, 4)
    vt = v_all.transpose(0, 2, 1, 3)
    scale = HEAD_DIM ** -0.5

    def block(qs):
        qr, qp = qs
        s = jnp.concatenate([jnp.einsum('bhmqd,bhmkd->bhmqk', qr, kr),
                             jnp.einsum('bhmqd,bhmkd->bhmqk', qp, kc)], axis=-1) * scale
        return diff_combine(s, lam, vt)

    o = lax.map(block, (to_blocks(q_rot), to_blocks(q_plain)))
    return o.transpose(1, 0, 3, 2, 4).reshape(bsz, t_len, DIFF_HEADS, DIFF_V)


def hybrid_mixer(h_lat, h_ctx, ang_r, ang_c, lam_init, ctx_out, w_in, w_out, mu, w0, w_up, a0, a_up,
                 g_up, k_k, k_a, r_k, ln_w, ln_b, lq1, lk1, lq2, lk2, subln_g):
    bsz, t_len = h_lat.shape[:2]
    l_ctx = h_ctx.shape[1]
    p_lat = h_lat @ w_in
    p_ctx = h_ctx @ w_in

    rc = rwkv_prepare(p_ctx[..., :RWKV_COLS], mu, w0, w_up, a0, a_up, g_up, k_k, k_a)
    rl = rwkv_prepare(p_lat[..., :RWKV_COLS], mu, w0, w_up, a0, a_up, g_up, k_k, k_a)
    s_zero = jnp.zeros((bsz, RWKV_HEADS, HEAD_DIM, HEAD_DIM), jnp.float32)
    yc, sf, sb = wkv_bidir(rc, (s_zero, s_zero))
    yl, _, _ = wkv_bidir(rl, (sf, sb))
    a_lat = rwkv_output(yl, rl, r_k, ln_w, ln_b)

    lam = (jnp.exp(jnp.sum(lq1 * lk1).astype(jnp.float32)) - jnp.exp(jnp.sum(lq2 * lk2).astype(jnp.float32))
           + lam_init)
    def qkv(p, n):
        d = p[..., RWKV_COLS:]
        q = d[..., :DIFF_QK].reshape(bsz, n, DIFF_HEADS, 2, HEAD_DIM)
        k = d[..., DIFF_QK:2 * DIFF_QK].reshape(bsz, n, DIFF_HEADS, 2, HEAD_DIM)
        v = d[..., 2 * DIFF_QK:].reshape(bsz, n, DIFF_HEADS, DIFF_V)
        return q, k, v
    ql, kl, vl = qkv(p_lat, t_len)
    qc, kc, vc = qkv(p_ctx, l_ctx)
    o_l = diff_attn_latent(rope_2d(ql, ang_r, ang_c), ql, rope_2d(kl, ang_r, ang_c), kc,
                           jnp.concatenate([vl, vc], axis=1), lam)
    subln = lambda o, n: (rmsnorm(o, subln_g) * (1.0 - lam_init)).reshape(bsz, n, DIFF_HEADS * DIFF_V)
    out_lat = jnp.concatenate([a_lat, subln(o_l, t_len)], axis=-1) @ w_out
    if not ctx_out:
        return out_lat, None
    a_ctx = rwkv_output(yc, rc, r_k, ln_w, ln_b)
    s_c = jnp.einsum('bqhmd,bkhmd->bhmqk', qc, kc) * (HEAD_DIM ** -0.5)
    o_c = diff_combine(s_c, lam, vc.transpose(0, 2, 1, 3)).transpose(0, 2, 1, 3)
    out_ctx = jnp.concatenate([a_ctx, subln(o_c, l_ctx)], axis=-1) @ w_out
    return out_lat, out_ctx


def ssd_scan(x, dt, a, bm, cm, h0):
    bsz, t_len = x.shape[:2]
    nc = t_len // CHUNK
    chunks = lambda t: jnp.moveaxis(t.reshape((bsz, nc, CHUNK) + t.shape[2:]), 1, 0)
    idx = jnp.arange(CHUNK)
    mask = (idx[:, None] >= idx[None, :])[:, :, None, None]

    def step(h, inp):
        xc, dtc, bc, cc = inp
        cs = jnp.cumsum(dtc * a, axis=1)
        seg = cs[:, :, None] - cs[:, None, :]
        lmat = jnp.exp(jnp.where(mask, seg, -jnp.inf))
        xdt = xc * dtc[..., None]
        cb = jnp.einsum('blgn,bsgn->blsg', cc, bc)
        y_diag = jnp.einsum('blsg,blsgh,bsghp->blghp', cb, lmat, xdt)
        y_off = jnp.einsum('blgn,bghpn->blghp', cc, h) * jnp.exp(cs)[..., None]
        h_new = (h * jnp.exp(cs[:, -1])[..., None, None]
                 + jnp.einsum('bsgn,bsgh,bsghp->bghpn', bc, jnp.exp(cs[:, -1:] - cs), xdt))
        return h_new, y_diag + y_off

    h, ys = lax.scan(step, h0, (chunks(x), chunks(dt), chunks(bm), chunks(cm)))
    return jnp.moveaxis(ys, 0, 1).reshape(x.shape), h


def ssd_mixer(h_lat, h_ctx, ctx_out, w_in, conv_w, conv_b, dt_bias, a_log, d_skip, norm_g, w_out):
    bsz = h_lat.shape[0]

    def prep(h):
        n = h.shape[1]
        p = h @ w_in
        z, xbc, dt = jnp.split(p, [D_INNER, D_INNER + CONV_DIM], axis=-1)
        xbc = jax.nn.silu(dwconv_centred(xbc, conv_w, conv_b))
        xs, bm, cm = jnp.split(xbc, [D_INNER, D_INNER + SSD_GROUPS * D_STATE], axis=-1)
        f = jnp.float32
        return (z, xs.reshape(bsz, n, SSD_GROUPS, SSD_HPG, SSD_HEAD_DIM).astype(f),
                bm.reshape(bsz, n, SSD_GROUPS, D_STATE).astype(f),
                cm.reshape(bsz, n, SSD_GROUPS, D_STATE).astype(f),
                dt.reshape(bsz, n, SSD_GROUPS, SSD_HPG).astype(f))

    def run(pr, h0s):
        _, xs, bm, cm, dt = pr
        y = d_skip.reshape(SSD_GROUPS, SSD_HPG).astype(jnp.float32)[..., None] * xs
        flip = lambda t: jnp.flip(t, axis=1)
        finals = []
        for d in range(2):
            dtd = jax.nn.softplus(dt + dt_bias[d].reshape(SSD_GROUPS, SSD_HPG).astype(jnp.float32))
            a = -jnp.exp(a_log[d].reshape(SSD_GROUPS, SSD_HPG).astype(jnp.float32))
            if d == 0:
                yd, hd = ssd_scan(xs, dtd, a, bm, cm, h0s[0])
            else:
                yd, hd = ssd_scan(flip(xs), flip(dtd), a, flip(bm), flip(cm), h0s[1])
                yd = flip(yd)
            y = y + yd
            finals.append(hd)
        return y, finals

    def out(y, z):
        n = z.shape[1]
        yg = y.reshape(bsz, n, D_INNER).astype(z.dtype) * jax.nn.silu(z)
        yn = rmsnorm(yg.reshape(bsz, n, SSD_GROUPS, D_INNER // SSD_GROUPS),
                     norm_g.reshape(SSD_GROUPS, D_INNER // SSD_GROUPS))
        return yn.reshape(bsz, n, D_INNER) @ w_out

    pc = prep(h_ctx)
    h_zero = jnp.zeros((bsz, SSD_GROUPS, SSD_HPG, SSD_HEAD_DIM, D_STATE), jnp.float32)
    yc, finals = run(pc, (h_zero, h_zero))
    pl = prep(h_lat)
    yl, _ = run(pl, finals)
    out_lat = out(yl, pl[0])
    out_ctx = out(yc, pc[0]) if ctx_out else None
    return out_lat, out_ctx


def sq_relu_mlp(h, w1, w2):
    return jnp.square(jax.nn.relu(h @ w1)) @ w2


def setup_inputs(seed: int = 0) -> dict:
    key = jax.random.key(seed)
    ks = iter(jax.random.split(key, 48))
    f32 = jnp.float32
    nrm = lambda shape, s: jax.random.normal(next(ks), shape, f32) * s
    D = D_MODEL
    dt0 = jnp.exp(jax.random.uniform(next(ks), (N_ODD, 2, SSD_HEADS), f32)
                  * (math.log(0.1) - math.log(0.001)) + math.log(0.001))
    return {
        "x": nrm((BATCH, SEQ, D), 1.0),
        "c": nrm((BATCH, D), 1.0),
        "ctx": nrm((BATCH, CTX_LEN, D), 1.0),
        "c_ctx": nrm((D,), 1.0),
        "ada_w": nrm((DEPTH, D, 6 * D), 0.5 * D ** -0.5),
        "ada_b": nrm((DEPTH, 6 * D), 0.02),
        "norm1_g": 1.0 + nrm((DEPTH, D), 0.02),
        "norm2_g": 1.0 + nrm((DEPTH, D), 0.02),
        "mlp_w1": nrm((DEPTH, D, FFN_DIM), D ** -0.5),
        "mlp_w2": nrm((DEPTH, FFN_DIM, D), FFN_DIM ** -0.5),
        "hy_w_in": nrm((N_EVEN, D, HYB_IN), D ** -0.5),
        "hy_w_out": nrm((N_EVEN, MIX_WIDTH, D), MIX_WIDTH ** -0.5),
        "rwkv_mu": jax.random.uniform(next(ks), (N_EVEN, RWKV_COLS), f32),
        "rwkv_w0": nrm((N_EVEN, 2, RWKV_DIM), 0.3),
        "rwkv_w_up": nrm((N_EVEN, 2, DECAY_LORA, RWKV_DIM), 0.5 * DECAY_LORA ** -0.5),
        "rwkv_a0": nrm((N_EVEN, 2, RWKV_DIM), 0.3),
        "rwkv_a_up": nrm((N_EVEN, 2, AAA_LORA, RWKV_DIM), 0.5 * AAA_LORA ** -0.5),
        "rwkv_g_up": nrm((N_EVEN, GATE_LORA, RWKV_DIM), GATE_LORA ** -0.5),
        "rwkv_k_k": 0.85 + nrm((N_EVEN, RWKV_DIM), 0.02),
        "rwkv_k_a": 1.0 + nrm((N_EVEN, RWKV_DIM), 0.02),
        "rwkv_r_k": nrm((N_EVEN, RWKV_HEADS, HEAD_DIM), 0.1),
        "rwkv_ln_w": 1.0 + nrm((N_EVEN, RWKV_DIM), 0.02),
        "rwkv_ln_b": nrm((N_EVEN, RWKV_DIM), 0.02),
        "diff_lq1": nrm((N_EVEN, HEAD_DIM), 0.1),
        "diff_lk1": nrm((N_EVEN, HEAD_DIM), 0.1),
        "diff_lq2": nrm((N_EVEN, HEAD_DIM), 0.1),
        "diff_lk2": nrm((N_EVEN, HEAD_DIM), 0.1),
        "diff_subln_g": 1.0 + nrm((N_EVEN, DIFF_V), 0.02),
        "ssd_w_in": nrm((N_ODD, D, SSD_IN), D ** -0.5),
        "ssd_conv_w": nrm((N_ODD, CONV_W, CONV_DIM), CONV_W ** -0.5),
        "ssd_conv_b": nrm((N_ODD, CONV_DIM), 0.02),
        "ssd_dt_bias": dt0 + jnp.log(-jnp.expm1(-dt0)),
        "ssd_a_log": jnp.log(jax.random.uniform(next(ks), (N_ODD, 2, SSD_HEADS), f32, 1.0, 16.0)),
        "ssd_d": 1.0 + nrm((N_ODD, SSD_HEADS), 0.1),
        "ssd_norm_g": 1.0 + nrm((N_ODD, D_INNER), 0.02),
        "ssd_w_out": nrm((N_ODD, D_INNER, D), D_INNER ** -0.5),
        "norm_f_g": 1.0 + nrm((D,), 0.02),
    }


def reference(x, c, ctx, c_ctx, ada_w, ada_b, norm1_g, norm2_g, mlp_w1, mlp_w2, hy_w_in, hy_w_out,
              rwkv_mu, rwkv_w0, rwkv_w_up, rwkv_a0, rwkv_a_up, rwkv_g_up, rwkv_k_k, rwkv_k_a, rwkv_r_k,
              rwkv_ln_w, rwkv_ln_b, diff_lq1, diff_lk1, diff_lq2, diff_lk2, diff_subln_g,
              ssd_w_in, ssd_conv_w, ssd_conv_b, ssd_dt_bias, ssd_a_log, ssd_d, ssd_norm_g, ssd_w_out,
              norm_f_g):
    t_len = x.shape[1]
    ang_r, ang_c = axial_angles(t_len)
    xl, xc = x, ctx
    for li in range(DEPTH):
        last = li == DEPTH - 1
        mod_l = jnp.split((jax.nn.silu(c) @ ada_w[li] + ada_b[li])[:, None, :], 6, axis=-1)
        mod_c = jnp.split(jax.nn.silu(c_ctx) @ ada_w[li] + ada_b[li], 6, axis=-1)
        hl = rmsnorm(xl, norm1_g[li]) * (1.0 + mod_l[1]) + mod_l[0]
        hc = rmsnorm(xc, norm1_g[li]) * (1.0 + mod_c[1]) + mod_c[0]
        if li % 2 == 0:
            e = li // 2
            lam_init = 0.8 - 0.6 * math.exp(-0.3 * li)
            ol, oc = hybrid_mixer(hl, hc, ang_r, ang_c, lam_init, not last, hy_w_in[e], hy_w_out[e],
                                  rwkv_mu[e], rwkv_w0[e], rwkv_w_up[e], rwkv_a0[e], rwkv_a_up[e],
                                  rwkv_g_up[e], rwkv_k_k[e], rwkv_k_a[e], rwkv_r_k[e], rwkv_ln_w[e],
                                  rwkv_ln_b[e], diff_lq1[e], diff_lk1[e], diff_lq2[e], diff_lk2[e],
                                  diff_subln_g[e])
        else:
            o = li // 2
            ol, oc = ssd_mixer(hl, hc, not last, ssd_w_in[o], ssd_conv_w[o], ssd_conv_b[o],
                               ssd_dt_bias[o], ssd_a_log[o], ssd_d[o], ssd_norm_g[o], ssd_w_out[o])
        xl = xl + mod_l[2] * ol
        hl = rmsnorm(xl, norm2_g[li]) * (1.0 + mod_l[4]) + mod_l[3]
        xl = xl + mod_l[5] * sq_relu_mlp(hl, mlp_w1[li], mlp_w2[li])
        if not last:
            xc = xc + mod_c[2] * oc
            hc = rmsnorm(xc, norm2_g[li]) * (1.0 + mod_c[4]) + mod_c[3]
            xc = xc + mod_c[5] * sq_relu_mlp(hc, mlp_w1[li], mlp_w2[li])
    return rmsnorm(xl, norm_f_g)
```

```python
import functools
import math

import jax
import jax.numpy as jnp
from jax import lax
from jax.experimental import pallas as pl
from jax.experimental.pallas import tpu as pltpu

F32 = jnp.float32
BF16 = jnp.bfloat16

D_MODEL = 1024
FFN_DIM = 4 * D_MODEL
NORM_EPS = 1e-6
GRID_W = 64
ROPE_BASE = 10000.0

HEAD_DIM = 64
RWKV_DIM = 512
RWKV_HEADS = 8
RWKV_COLS = 1792
RWKV_GN_EPS = 64e-5
DIFF_HEADS = 4
DIFF_V = 128
DIFF_W = 512
HYB_IN = 3328

D_INNER = 2048
SSD_HEADS = 32
SSD_GROUPS = 8
SSD_HPG = 4
D_STATE = 128
CONV_W = 5
CONV_DIM = 4096
CHUNK = 128

TM = 256
HALO = 8
FFN_CHUNK = 512
WKV_TBLK = 128
VMEM_LIMIT = 56 * 1024 * 1024


def _cparams(sem):
    return pltpu.CompilerParams(dimension_semantics=sem, vmem_limit_bytes=VMEM_LIMIT)


def _const_spec(shape):
    nd = len(shape)
    return pl.BlockSpec(shape, lambda *_: (0,) * nd, pipeline_mode=pl.Buffered(1))


def _split2(x):
    hi = x.astype(BF16)
    lo = (x - hi.astype(F32)).astype(BF16)
    return hi, lo


def _segsum(x, ones):
    hi, lo = _split2(x)
    return (jnp.dot(hi, ones, preferred_element_type=F32)
            + jnp.dot(lo, ones, preferred_element_type=F32))


def _rms_mod(x, g, shift, scale):
    ms = jnp.mean(x * x, axis=-1, keepdims=True)
    return (x * lax.rsqrt(ms + NORM_EPS) * g) * (1.0 + scale) + shift


def _sigmoid(x):
    return 1.0 / (1.0 + jnp.exp(-x))


def _softplus(x):
    return jnp.maximum(x, 0.0) + jnp.log(1.0 + jnp.exp(-jnp.abs(x)))


def _silu(x):
    return x * _sigmoid(x)


def _ada_kernel(c_ref, w_ref, b_ref, o_ref):
    o_ref[0] = jnp.dot(_silu(c_ref[...]), w_ref[0], precision=lax.Precision.HIGHEST,
                       preferred_element_type=F32) + b_ref[0]


def _ada(cc, ada_w, ada_b):
    depth, d, n = ada_w.shape
    tn = 1536
    return pl.pallas_call(
        _ada_kernel,
        grid=(depth, n // tn),
        in_specs=[_const_spec(cc.shape),
                  pl.BlockSpec((1, d, tn), lambda l, j: (l, 0, j)),
                  pl.BlockSpec((1, 1, tn), lambda l, j: (l, 0, j))],
        out_specs=pl.BlockSpec((1, cc.shape[0], tn), lambda l, j: (l, 0, j)),
        out_shape=jax.ShapeDtypeStruct((depth, cc.shape[0], n), F32),
        compiler_params=_cparams(("parallel", "parallel")),
        name="ada",
    )(cc, ada_w, ada_b.reshape(depth, 1, n))


def _halo_specs(n_lat_tiles, width):
    per = TM // HALO
    last = (n_lat_tiles + 1) * per - 1
    prev = pl.BlockSpec((1, HALO, width), lambda b, i: (b, jnp.maximum(i * per - 1, 0), 0))
    cur = pl.BlockSpec((1, TM, width), lambda b, i: (b, i, 0))
    nxt = pl.BlockSpec((1, HALO, width), lambda b, i: (b, jnp.minimum((i + 1) * per, last), 0))
    return prev, cur, nxt


def _mod_spec(n_lat_tiles):
    return pl.BlockSpec((1, 6, D_MODEL), lambda b, i: (b * 2 + (i >= n_lat_tiles).astype(jnp.int32), 0, 0))


def _normed_with_halo(xp_ref, xc_ref, xn_ref, g_ref, mod_ref, hbuf, n_lat_tiles):
    i = pl.program_id(1)
    g = g_ref[...]
    shift, scale = mod_ref[0, 0:1, :], mod_ref[0, 1:2, :]
    prev_ok = jnp.logical_and(i >= 1, i < n_lat_tiles).astype(F32)
    next_ok = (i < n_lat_tiles - 1).astype(F32)
    hbuf[0:HALO, :] = _rms_mod(xp_ref[0], g, shift, scale) * prev_ok
    hbuf[HALO:HALO + TM, :] = _rms_mod(xc_ref[0], g, shift, scale)
    hbuf[HALO + TM:, :] = _rms_mod(xn_ref[0], g, shift, scale) * next_ok


def _hyb_in_kernel(xp_ref, xc_ref, xn_ref, g_ref, mod_ref, w_ref, mu_ref, kk_w_ref, ka_ref,
                   w0_ref, a0_ref, lora_ref, gup_ref, ones_ref, cos_ref, sin_ref,
                   r_out, v_out, a_out, dec_out, km_out, bb_out, g_out,
                   qr_out, qp_out, ks_out, vd_out, hbuf, pbuf, *, n_lat_tiles):
    _normed_with_halo(xp_ref, xc_ref, xn_ref, g_ref, mod_ref, hbuf, n_lat_tiles)
    pbuf[...] = jnp.dot(hbuf[...].astype(BF16), w_ref[...], preferred_element_type=F32)

    p = pbuf[HALO:HALO + TM, 0:RWKV_COLS]
    prev = pbuf[pl.ds(HALO - 1, TM), 0:RWKV_COLS]
    nxt = pbuf[pl.ds(HALO + 1, TM), 0:RWKV_COLS]
    p = p + (0.5 * (prev + nxt) - p) * mu_ref[...]
    r = p[:, 0:RWKV_DIM]
    k = p[:, RWKV_DIM:2 * RWKV_DIM]
    v = p[:, 2 * RWKV_DIM:3 * RWKV_DIM]
    lo_in = p[:, 3 * RWKV_DIM:3 * RWKV_DIM + 128]
    gd = p[:, 3 * RWKV_DIM + 128:RWKV_COLS]
    kkf = k * kk_w_ref[...]
    kk = kkf * lax.rsqrt(_segsum(kkf * kkf, ones_ref[...]) + 1e-12)
    r_out[0] = r
    v_out[0] = v
    a_out[0] = -kk
    g_out[0] = jnp.dot(_sigmoid(gd).astype(BF16), gup_ref[...], preferred_element_type=F32)
    lane = lax.broadcasted_iota(jnp.int32, lo_in.shape, 1)
    lo_act = jnp.where(lane < 64, jnp.tanh(lo_in), lo_in).astype(BF16)
    for d in range(2):
        up = jnp.dot(lo_act, lora_ref[d], preferred_element_type=F32)
        wlog = -_softplus(-(w0_ref[d:d + 1, :] + up[:, 0:RWKV_DIM])) - 0.5
        dec_out[d, 0] = jnp.exp(-jnp.exp(wlog))
        a = _sigmoid(a0_ref[d:d + 1, :] + up[:, RWKV_DIM:])
        km_out[d, 0] = k * (1.0 + (a - 1.0) * ka_ref[...])
        bb_out[d, 0] = kk * a

    o = RWKV_COLS
    vd_out[0] = pbuf[HALO:HALO + TM, o + 2 * DIFF_W:o + 3 * DIFF_W].astype(BF16)
    lane = lax.broadcasted_iota(jnp.int32, (TM, 128), 1)
    first = (lane % 32) < 16
    for s in range(DIFF_W // 128):
        ls = slice(s * 128, (s + 1) * 128)
        cos, sin = cos_ref[:, ls], sin_ref[:, ls]

        def rope(t):
            partner = jnp.where(first, pltpu.roll(t, 128 - 16, 1), pltpu.roll(t, 16, 1))
            return t * cos + partner * sin

        q = pbuf[HALO:HALO + TM, o + s * 128:o + (s + 1) * 128] * (HEAD_DIM ** -0.5)
        kd = pbuf[HALO:HALO + TM, o + DIFF_W + s * 128:o + DIFF_W + (s + 1) * 128]
        qp_out[0, :, ls] = q.astype(BF16)
        qr_out[0, :, ls] = rope(q).astype(BF16)
        ks_out[0, :, ls] = rope(kd).astype(BF16)


def _hyb_in(x, norm_g, mod, w_in, mu, k_k, k_a, w0, a0, lora, g_up, ones64, cos_t, sin_t, n_lat_tiles):
    bsz, n_tok, _ = x.shape
    n_tiles = n_tok // TM
    prev, cur, nxt = _halo_specs(n_lat_tiles, D_MODEL)
    tile = lambda w, dt: jax.ShapeDtypeStruct((bsz, n_tok, w), dt)
    tile2 = lambda w, dt: jax.ShapeDtypeStruct((2, bsz, n_tok, w), dt)
    o1 = pl.BlockSpec((1, TM, RWKV_DIM), lambda b, i: (b, i, 0))
    o2 = pl.BlockSpec((2, 1, TM, RWKV_DIM), lambda b, i: (0, b, i, 0))
    tab = pl.BlockSpec((TM, DIFF_W), lambda b, i: (i, 0))
    return pl.pallas_call(
        functools.partial(_hyb_in_kernel, n_lat_tiles=n_lat_tiles),
        grid=(bsz, n_tiles),
        in_specs=[prev, cur, nxt, _const_spec((1, D_MODEL)), _mod_spec(n_lat_tiles),
                  _const_spec(w_in.shape), _const_spec(mu.shape), _const_spec(k_k.shape),
                  _const_spec(k_a.shape), _const_spec(w0.shape), _const_spec(a0.shape),
                  _const_spec(lora.shape), _const_spec(g_up.shape), _const_spec(ones64.shape), tab, tab],
        out_specs=[o1, o1, o1, o2, o2, o2, o1, o1, o1, o1, o1],
        out_shape=[tile(RWKV_DIM, F32)] * 3 + [tile2(RWKV_DIM, F32)] * 3 + [tile(RWKV_DIM, F32)]
                  + [tile(DIFF_W, BF16)] * 4,
        scratch_shapes=[pltpu.VMEM((TM + 2 * HALO, D_MODEL), F32),
                        pltpu.VMEM((TM + 2 * HALO, HYB_IN), F32)],
        compiler_params=_cparams(("parallel", "parallel")),
        name="hyb_in",
    )(x, x, x, norm_g, mod, w_in, mu, k_k, k_a, w0, a0, lora, g_up, ones64, cos_t, sin_t)


def _wkv_kernel(r_ref, v_ref, a_ref, w_ref, k_ref, b_ref, y_ref, s_ref, *, gb, reverse):
    @pl.when(pl.program_id(1) == 0)
    def _():
        s_ref[...] = jnp.zeros_like(s_ref)

    lane = lax.broadcasted_iota(jnp.int32, (HEAD_DIM, 128), 1)
    sub = lax.broadcasted_iota(jnp.int32, (HEAD_DIM, 128), 0)
    sel = (lane % HEAD_DIM == sub).astype(F32)
    row = lax.broadcasted_iota(jnp.int32, (128, 128), 0)
    col = lax.broadcasted_iota(jnp.int32, (128, 128), 1)
    ones = (row // HEAD_DIM == col // HEAD_DIM).astype(BF16)
    sub8 = lax.broadcasted_iota(jnp.int32, (8, 128), 0)
    n_grp = WKV_TBLK // 8

    def group(i, carry):
        base = pl.multiple_of((n_grp - 1 - i if reverse else i) * 8, 8)
        for bb in range(gb):
            for p in range(RWKV_DIM // 128):
                c = bb * (RWKV_DIM // 128) + p
                sl = slice(p * 128, (p + 1) * 128)
                tiles = [ref[bb, pl.ds(base, 8), sl] for ref in (a_ref, v_ref, r_ref)]
                tiles += [ref[0, bb, pl.ds(base, 8), sl] for ref in (w_ref, k_ref, b_ref)]
                y_tile = jnp.zeros((8, 128), F32)
                s = s_ref[c]
                for tt in (reversed(range(8)) if reverse else range(8)):
                    a_t, v_t, r_t, w_t, k_t, b_t = [t[tt:tt + 1, :] for t in tiles]
                    sa = _segsum(s * a_t, ones)
                    vb = _segsum(sel * v_t, ones)
                    s = s * w_t + sa * b_t + vb * k_t
                    yb = _segsum(s * r_t, ones)
                    y_row = jnp.sum(yb * sel, axis=0, keepdims=True)
                    y_tile = jnp.where(sub8 == tt, y_row, y_tile)
                s_ref[c] = s
                y_ref[bb, pl.ds(base, 8), sl] = y_tile
        return carry

    lax.fori_loop(0, n_grp, group, 0)


def _wkv(r, v, a, dec, km, bb, n_lat_tok, reverse):
    bsz, n_tok, _ = r.shape
    gb = 2 if bsz % 2 == 0 else 1
    n_blk = n_tok // WKV_TBLK
    n_lat = n_lat_tok // WKV_TBLK
    n_ctx = n_blk - n_lat
    d = 1 if reverse else 0

    def tb(j):
        return n_blk - 1 - j if reverse else jnp.where(j < n_ctx, n_lat + j, j - n_ctx)

    s1 = pl.BlockSpec((gb, WKV_TBLK, RWKV_DIM), lambda g, j: (g, tb(j), 0))
    s2 = pl.BlockSpec((1, gb, WKV_TBLK, RWKV_DIM), lambda g, j: (d, g, tb(j), 0))
    return pl.pallas_call(
        functools.partial(_wkv_kernel, gb=gb, reverse=reverse),
        grid=(bsz // gb, n_blk),
        in_specs=[s1, s1, s1, s2, s2, s2],
        out_specs=s1,
        out_shape=jax.ShapeDtypeStruct((bsz, n_tok, RWKV_DIM), F32),
        scratch_shapes=[pltpu.VMEM((gb * (RWKV_DIM // 128), HEAD_DIM, 128), F32)],
        compiler_params=_cparams(("parallel", "arbitrary")),
        name="wkv_bwd" if reverse else "wkv_fwd",
    )(r, v, a, dec, km, bb)


def _attn_kernel(lam_ref, qr_ref, qp_ref, kl_ref, kc_ref, vl_ref, vc_ref, o_ref, *, latent):
    lam = lam_ref[0]
    lane = lax.broadcasted_iota(jnp.int32, (1, 128), 1)
    nt = (((1,), (1,)), ((), ()))
    acc_l, acc_c = None, None
    for m in range(2):
        own = (lane // HEAD_DIM) == m
        qp = jnp.where(own, qp_ref[0], jnp.zeros_like(qp_ref[0]))
        sc = lax.dot_general(qp, kc_ref[0], nt, preferred_element_type=F32)
        mx = jnp.max(sc, axis=-1, keepdims=True)
        if latent:
            qr = jnp.where(own, qr_ref[0], jnp.zeros_like(qr_ref[0]))
            sl = lax.dot_general(qr, kl_ref[0], nt, preferred_element_type=F32)
            mx = jnp.maximum(mx, jnp.max(sl, axis=-1, keepdims=True))
            el = jnp.exp(sl - mx)
        ec = jnp.exp(sc - mx)
        den = jnp.sum(ec, axis=-1, keepdims=True)
        if latent:
            den = den + jnp.sum(el, axis=-1, keepdims=True)
        coef = (1.0 / den) * (1.0 if m == 0 else -lam)
        acc_c = ec * coef if acc_c is None else acc_c + ec * coef
        if latent:
            acc_l = el * coef if acc_l is None else acc_l + el * coef
    o = jnp.dot(acc_c.astype(BF16), vc_ref[0], preferred_element_type=F32)
    if latent:
        o = o + jnp.dot(acc_l.astype(BF16), vl_ref[0], preferred_element_type=F32)
    o_ref[0] = o


def _attn(lam, qr, qp, ks, vd, n_lat_tok, latent):
    bsz, n_tok, _ = qr.shape
    n_ctx_tok = n_tok - n_lat_tok
    ctx_blk = n_lat_tok // n_ctx_tok
    if latent:
        qb, n_q, q_off = TM, n_lat_tok // TM, 0
    else:
        qb, n_q, q_off = n_ctx_tok, 1, ctx_blk
    qspec = pl.BlockSpec((1, qb, 128), lambda b, h, i: (b, i + q_off, h))
    lat = pl.BlockSpec((1, n_lat_tok, 128), lambda b, h, i: (b, 0, h))
    ctx = pl.BlockSpec((1, n_ctx_tok, 128), lambda b, h, i: (b, ctx_blk, h))
    return pl.pallas_call(
        functools.partial(_attn_kernel, latent=latent),
        grid=(bsz, DIFF_HEADS, n_q),
        in_specs=[pl.BlockSpec(memory_space=pltpu.SMEM), qspec, qspec, lat, ctx, lat, ctx],
        out_specs=pl.BlockSpec((1, qb, 128), lambda b, h, i: (b, i, h)),
        out_shape=jax.ShapeDtypeStruct((bsz, n_q * qb, DIFF_W), F32),
        compiler_params=_cparams(("parallel", "parallel", "arbitrary")),
        name="attn_lat" if latent else "attn_ctx",
    )(lam, qr, qp, ks, ks, vd, vd)


def _hyb_out_kernel(x_ref, mod_ref, yf_ref, yb_ref, r_ref, v_ref, km_ref, g_ref, ol_ref, oc_ref, ones_ref,
                    rk_ref, lnw_ref, lnb_ref, sg_ref, wo_ref, out_ref, *, n_lat_tiles, sub_scale):
    i = pl.program_id(1)
    ones = ones_ref[...]
    y = yf_ref[0] + yb_ref[0]
    mean = _segsum(y, ones) * (1.0 / HEAD_DIM)
    yc = y - mean
    var = _segsum(yc * yc, ones) * (1.0 / HEAD_DIM)
    yn = yc * lax.rsqrt(var + RWKV_GN_EPS) * lnw_ref[...] + lnb_ref[...]
    r = r_ref[0]
    kb = 0.5 * (km_ref[0, 0] + km_ref[1, 0])
    bonus = _segsum(r * kb * rk_ref[...], ones) * v_ref[0]
    a_mix = ((yn + bonus) * g_ref[0]).astype(BF16)

    o = jnp.where(i < n_lat_tiles, ol_ref[0], oc_ref[0])
    subs = []
    for h in range(DIFF_HEADS):
        oh = o[:, h * DIFF_V:(h + 1) * DIFF_V]
        ms = jnp.mean(oh * oh, axis=-1, keepdims=True)
        subs.append(oh * lax.rsqrt(ms + NORM_EPS) * sg_ref[...] * sub_scale)
    sub = jnp.concatenate(subs, axis=1).astype(BF16)
    mix = (jnp.dot(a_mix, wo_ref[0:RWKV_DIM, :], preferred_element_type=F32)
           + jnp.dot(sub, wo_ref[RWKV_DIM:, :], preferred_element_type=F32))
    out_ref[0] = x_ref[0] + mod_ref[0, 2:3, :] * mix


def _hyb_out(x, mod, yf, yb, r, v, km, g, o_lat, o_ctx, ones64, r_k, ln_w, ln_b, subln_g, w_out,
             n_lat_tiles, sub_scale):
    bsz, n_tok, _ = x.shape
    n_tiles = n_tok // TM
    t1 = lambda w: pl.BlockSpec((1, TM, w), lambda b, i: (b, i, 0))
    t2 = pl.BlockSpec((2, 1, TM, RWKV_DIM), lambda b, i: (0, b, i, 0))
    ol = pl.BlockSpec((1, TM, DIFF_W), lambda b, i: (b, jnp.minimum(i, n_lat_tiles - 1), 0))
    oc = pl.BlockSpec((1, TM, DIFF_W), lambda b, i: (b, 0, 0))
    return pl.pallas_call(
        functools.partial(_hyb_out_kernel, n_lat_tiles=n_lat_tiles, sub_scale=sub_scale),
        grid=(bsz, n_tiles),
        in_specs=[t1(D_MODEL), _mod_spec(n_lat_tiles), t1(RWKV_DIM), t1(RWKV_DIM), t1(RWKV_DIM), t1(RWKV_DIM), t2,
                  t1(RWKV_DIM), ol, oc, _const_spec(ones64.shape), _const_spec(r_k.shape),
                  _const_spec(ln_w.shape), _const_spec(ln_b.shape), _const_spec(subln_g.shape),
                  _const_spec(w_out.shape)],
        out_specs=t1(D_MODEL),
        out_shape=jax.ShapeDtypeStruct(x.shape, F32),
        compiler_params=_cparams(("parallel", "parallel")),
        name="hyb_out",
    )(x, mod, yf, yb, r, v, km, g, o_lat, o_ctx, ones64, r_k, ln_w, ln_b, subln_g, w_out)


def _mlp_kernel(x_ref, mod_ref, g_ref, w1_ref, w2_ref, gf_ref, out_ref, *, final_norm):
    x = x_ref[0]
    h = _rms_mod(x, g_ref[...], mod_ref[0, 3:4, :], mod_ref[0, 4:5, :]).astype(BF16)
    acc = jnp.zeros((TM, D_MODEL), F32)
    for c in range(FFN_DIM // FFN_CHUNK):
        cs = slice(c * FFN_CHUNK, (c + 1) * FFN_CHUNK)
        u = jnp.maximum(jnp.dot(h, w1_ref[:, cs], preferred_element_type=F32), 0.0)
        acc = acc + jnp.dot((u * u).astype(BF16), w2_ref[cs, :], preferred_element_type=F32)
    out = x + mod_ref[0, 5:6, :] * acc
    if final_norm:
        ms = jnp.mean(out * out, axis=-1, keepdims=True)
        out = out * lax.rsqrt(ms + NORM_EPS) * gf_ref[...]
    out_ref[0] = out


def _mlp(x, mod, norm_g, w1, w2, norm_f, n_lat_tiles, final_norm):
    bsz, n_tok, _ = x.shape
    t1 = pl.BlockSpec((1, TM, D_MODEL), lambda b, i: (b, i, 0))
    return pl.pallas_call(
        functools.partial(_mlp_kernel, final_norm=final_norm),
        grid=(bsz, n_tok // TM),
        in_specs=[t1, _mod_spec(n_lat_tiles), _const_spec(norm_g.shape), _const_spec(w1.shape),
                  _const_spec(w2.shape), _const_spec(norm_f.shape)],
        out_specs=t1,
        out_shape=jax.ShapeDtypeStruct(x.shape, F32),
        compiler_params=_cparams(("parallel", "parallel")),
        name="mlp",
    )(x, mod, norm_g, w1, w2, norm_f)


def _ssd_in_kernel(xp_ref, xc_ref, xn_ref, g_ref, mod_ref, wzx_ref, wdt_ref, cw_ref, cb_ref,
                   z_out, xs_out, bm_out, cm_out, dt_out, hbuf, pbuf, *, n_lat_tiles):
    _normed_with_halo(xp_ref, xc_ref, xn_ref, g_ref, mod_ref, hbuf, n_lat_tiles)
    hc = hbuf[HALO:HALO + TM, :].astype(BF16)
    z_out[0] = jnp.dot(hc, wzx_ref[:, 0:D_INNER], preferred_element_type=F32)
    dt_out[0] = jnp.dot(hc, wdt_ref[...], preferred_element_type=F32)
    pbuf[...] = jnp.dot(hbuf[...].astype(BF16), wzx_ref[:, D_INNER:], preferred_element_type=F32)
    conv = cb_ref[...] + cw_ref[CONV_W // 2:CONV_W // 2 + 1, :] * pbuf[HALO:HALO + TM, :]
    for j in range(CONV_W):
        if j != CONV_W // 2:
            conv = conv + cw_ref[j:j + 1, :] * pbuf[pl.ds(HALO + j - CONV_W // 2, TM), :]
    act = _silu(conv)
    xs_out[0] = act[:, 0:D_INNER]
    bm_out[0] = act[:, D_INNER:D_INNER + SSD_GROUPS * D_STATE].astype(BF16)
    cm_out[0] = act[:, D_INNER + SSD_GROUPS * D_STATE:].astype(BF16)


def _ssd_in(x, norm_g, mod, w_zx, w_dt, conv_w, conv_b, n_lat_tiles):
    bsz, n_tok, _ = x.shape
    prev, cur, nxt = _halo_specs(n_lat_tiles, D_MODEL)
    t1 = lambda w: pl.BlockSpec((1, TM, w), lambda b, i: (b, i, 0))
    sh = lambda w, dt: jax.ShapeDtypeStruct((bsz, n_tok, w), dt)
    gw = SSD_GROUPS * D_STATE
    return pl.pallas_call(
        functools.partial(_ssd_in_kernel, n_lat_tiles=n_lat_tiles),
        grid=(bsz, n_tok // TM),
        in_specs=[prev, cur, nxt, _const_spec((1, D_MODEL)), _mod_spec(n_lat_tiles),
                  _const_spec(w_zx.shape), _const_spec(w_dt.shape), _const_spec(conv_w.shape),
                  _const_spec(conv_b.shape)],
        out_specs=[t1(D_INNER), t1(D_INNER), t1(gw), t1(gw), t1(128)],
        out_shape=[sh(D_INNER, F32), sh(D_INNER, F32), sh(gw, BF16), sh(gw, BF16), sh(128, F32)],
        scratch_shapes=[pltpu.VMEM((TM + 2 * HALO, D_MODEL), F32),
                        pltpu.VMEM((TM + 2 * HALO, CONV_DIM), F32)],
        compiler_params=_cparams(("parallel", "parallel")),
        name="ssd_in",
    )(x, x, x, norm_g, mod, w_zx, w_dt, conv_w, conv_b)


def _ssd_scan_kernel(xs_ref, bm_ref, cm_ref, dt_ref, dtb_ref, alog_ref, y_ref, h_ref):
    d = pl.program_id(0)
    j = pl.program_id(2)

    @pl.when(j == 0)
    def _():
        h_ref[...] = jnp.zeros_like(h_ref)

    row = lax.broadcasted_iota(jnp.int32, (CHUNK, CHUNK), 0)
    col = lax.broadcasted_iota(jnp.int32, (CHUNK, CHUNK), 1)
    keep = jnp.where(d == 0, row - col, col - row) >= 0
    dtd = _softplus(dt_ref[0] + dtb_ref[0])
    dta = dtd * (-jnp.exp(alog_ref[0]))
    cs = jnp.dot(keep.astype(F32), dta, precision=lax.Precision.HIGHEST, preferred_element_type=F32)
    cs_t = cs.T
    tot = jnp.where(d == 0, cs[CHUNK - 1:CHUNK, :], cs[0:1, :])
    e_cs = jnp.exp(cs)
    e_end = jnp.exp(tot - cs)
    e_tot = jnp.exp(tot)
    nt = (((1,), (1,)), ((), ()))
    for g in range(SSD_GROUPS):
        bc = bm_ref[0, :, g * D_STATE:(g + 1) * D_STATE]
        cc = cm_ref[0, :, g * D_STATE:(g + 1) * D_STATE]
        cb = lax.dot_general(cc, bc, nt, preferred_element_type=F32)
        for hh in range(SSD_HPG):
            h = g * SSD_HPG + hh
            hs = slice(h * HEAD_DIM, (h + 1) * HEAD_DIM)
            seg = cs[:, h:h + 1] - cs_t[h:h + 1, :]
            lmat = jnp.exp(jnp.where(keep, seg, -jnp.inf))
            xdt = xs_ref[0, :, hs] * dtd[:, h:h + 1]
            state = h_ref[h]
            y = jnp.dot((cb * lmat).astype(BF16), xdt.astype(BF16), preferred_element_type=F32)
            y = y + lax.dot_general(cc, state.astype(BF16), nt, preferred_element_type=F32) * e_cs[:, h:h + 1]
            y_ref[0, 0, :, hs] = y
            xw_t = (xdt * e_end[:, h:h + 1]).T.astype(BF16)
            h_ref[h] = state * e_tot[:, h:h + 1] + jnp.dot(xw_t, bc, preferred_element_type=F32)


def _ssd_scan(xs, bm, cm, dt, dt_bias, a_log, n_lat_tok):
    bsz, n_tok, _ = xs.shape
    n_blk = n_tok // CHUNK
    n_lat = n_lat_tok // CHUNK
    n_ctx = n_blk - n_lat

    def tb(d, j):
        fwd = jnp.where(j < n_ctx, n_lat + j, j - n_ctx)
        return jnp.where(d == 0, fwd, n_blk - 1 - j)

    t1 = lambda w: pl.BlockSpec((1, CHUNK, w), lambda d, b, j: (b, tb(d, j), 0))
    par = pl.BlockSpec((1, 1, 128), lambda d, b, j: (d, 0, 0))
    return pl.pallas_call(
        _ssd_scan_kernel,
        grid=(2, bsz, n_blk),
        in_specs=[t1(D_INNER), t1(SSD_GROUPS * D_STATE), t1(SSD_GROUPS * D_STATE), t1(128), par, par],
        out_specs=pl.BlockSpec((1, 1, CHUNK, D_INNER), lambda d, b, j: (d, b, tb(d, j), 0)),
        out_shape=jax.ShapeDtypeStruct((2, bsz, n_tok, D_INNER), F32),
        scratch_shapes=[pltpu.VMEM((SSD_HEADS, HEAD_DIM, D_STATE), F32)],
        compiler_params=_cparams(("parallel", "parallel", "arbitrary")),
        name="ssd_scan",
    )(xs, bm, cm, dt, dt_bias, a_log)


def _ssd_out_kernel(x_ref, mod_ref, y_ref, xs_ref, z_ref, dsk_ref, ng_ref, wo_ref, out_ref):
    y = y_ref[0, 0] + y_ref[1, 0] + dsk_ref[...] * xs_ref[0]
    yg = y * _silu(z_ref[0])
    gw = D_INNER // SSD_GROUPS
    parts = []
    for g in range(SSD_GROUPS):
        t = yg[:, g * gw:(g + 1) * gw]
        ms = jnp.mean(t * t, axis=-1, keepdims=True)
        parts.append((t * lax.rsqrt(ms + NORM_EPS) * ng_ref[:, g * gw:(g + 1) * gw]).astype(BF16))
    yn = jnp.concatenate(parts, axis=1)
    out_ref[0] = x_ref[0] + mod_ref[0, 2:3, :] * jnp.dot(yn, wo_ref[...], preferred_element_type=F32)


def _ssd_out(x, mod, y, xs, z, d_skip, norm_g, w_out, n_lat_tiles):
    bsz = x.shape[0]
    t1 = lambda w: pl.BlockSpec((1, TM, w), lambda b, i: (b, i, 0))
    return pl.pallas_call(
        _ssd_out_kernel,
        grid=(bsz, n_lat_tiles),
        in_specs=[t1(D_MODEL), _mod_spec(n_lat_tiles),
                  pl.BlockSpec((2, 1, TM, D_INNER), lambda b, i: (0, b, i, 0)),
                  t1(D_INNER), t1(D_INNER), _const_spec(d_skip.shape), _const_spec(norm_g.shape),
                  _const_spec(w_out.shape)],
        out_specs=t1(D_MODEL),
        out_shape=jax.ShapeDtypeStruct((bsz, n_lat_tiles * TM, D_MODEL), F32),
        compiler_params=_cparams(("parallel", "parallel")),
        name="ssd_out",
    )(x, mod, y, xs, z, d_skip, norm_g, w_out)


def _rope_tables(t_len, n_ctx_tok):
    n_rows = t_len // GRID_W
    rows = jnp.broadcast_to(jnp.arange(n_rows)[:, None], (n_rows, GRID_W)).reshape(-1)
    cols = jnp.broadcast_to(jnp.arange(GRID_W)[None, :], (n_rows, GRID_W)).reshape(-1)
    inv = ROPE_BASE ** (-jnp.arange(0, HEAD_DIM // 2, 2, dtype=F32) / (HEAD_DIM // 2))
    ang_r = rows.astype(F32)[:, None] * inv
    ang_c = cols.astype(F32)[:, None] * inv
    cos = jnp.concatenate([jnp.cos(ang_r)] * 2 + [jnp.cos(ang_c)] * 2, axis=1)
    sin = jnp.concatenate([-jnp.sin(ang_r), jnp.sin(ang_r), -jnp.sin(ang_c), jnp.sin(ang_c)], axis=1)
    reps = DIFF_W // HEAD_DIM
    cos = jnp.concatenate([jnp.tile(cos, (1, reps)), jnp.ones((n_ctx_tok, DIFF_W), F32)], axis=0)
    sin = jnp.concatenate([jnp.tile(sin, (1, reps)), jnp.zeros((n_ctx_tok, DIFF_W), F32)], axis=0)
    return cos, sin


def kernel(x, c, ctx, c_ctx, ada_w, ada_b, norm1_g, norm2_g, mlp_w1, mlp_w2, hy_w_in, hy_w_out, rwkv_mu, rwkv_w0, rwkv_w_up, rwkv_a0, rwkv_a_up, rwkv_g_up, rwkv_k_k, rwkv_k_a, rwkv_r_k, rwkv_ln_w, rwkv_ln_b, diff_lq1, diff_lk1, diff_lq2, diff_lk2, diff_subln_g, ssd_w_in, ssd_conv_w, ssd_conv_b, ssd_dt_bias, ssd_a_log, ssd_d, ssd_norm_g, ssd_w_out, norm_f_g):
    bsz, t_len, _ = x.shape
    l_ctx = ctx.shape[1]
    assert l_ctx == TM and t_len % TM == 0 and t_len % l_ctx == 0
    depth = ada_w.shape[0]
    n_lat_tiles = t_len // TM
    row = lambda t: t.reshape(1, -1)

    cc = jnp.zeros((16, D_MODEL), F32).at[0:bsz].set(c).at[bsz].set(c_ctx)
    mods = _ada(cc, ada_w, ada_b)
    mod_lat = mods[:, 0:bsz]
    mod_ctx = jnp.broadcast_to(mods[:, bsz:bsz + 1], mod_lat.shape)
    mods = jnp.stack([mod_lat, mod_ctx], axis=2).reshape(depth, bsz * 2, 6, D_MODEL)

    xa = jnp.concatenate([x, ctx], axis=1)
    ones64 = (jnp.arange(RWKV_DIM)[:, None] // HEAD_DIM == jnp.arange(RWKV_DIM)[None, :] // HEAD_DIM).astype(BF16)
    cos_t, sin_t = _rope_tables(t_len, l_ctx)

    for li in range(depth):
        last = li == depth - 1
        mod = mods[li]
        if li % 2 == 0:
            e = li // 2
            lam_init = 0.8 - 0.6 * math.exp(-0.3 * li)
            lam = (jnp.exp(jnp.sum(diff_lq1[e] * diff_lk1[e])) - jnp.exp(jnp.sum(diff_lq2[e] * diff_lk2[e]))
                   + lam_init).reshape(1).astype(F32)
            zero = jnp.zeros((64, RWKV_DIM), F32)
            lora = jnp.stack([jnp.concatenate(
                [jnp.concatenate([rwkv_w_up[e, d], zero], axis=1),
                 jnp.concatenate([zero, rwkv_a_up[e, d]], axis=1)], axis=0) for d in range(2)]).astype(BF16)
            (r, v, a, dec, km, bb, g, qr, qp, ks, vd) = _hyb_in(
                xa, row(norm1_g[li]), mod, hy_w_in[e].astype(BF16), row(rwkv_mu[e]), row(rwkv_k_k[e]),
                row(rwkv_k_a[e]), rwkv_w0[e], rwkv_a0[e], lora, rwkv_g_up[e].astype(BF16), ones64,
                cos_t, sin_t, n_lat_tiles)
            yf = _wkv(r, v, a, dec, km, bb, t_len, False)
            yb = _wkv(r, v, a, dec, km, bb, t_len, True)
            o_lat = _attn(lam, qr, qp, ks, vd, t_len, True)
            o_ctx = _attn(lam, qr, qp, ks, vd, t_len, False)
            xa = _hyb_out(xa, mod, yf, yb, r, v, km, g, o_lat, o_ctx, ones64, row(rwkv_r_k[e]),
                          row(rwkv_ln_w[e]), row(rwkv_ln_b[e]),
                          row(diff_subln_g[e]), hy_w_out[e].astype(BF16), n_lat_tiles, 1.0 - lam_init)
        else:
            o = li // 2
            w_in = ssd_w_in[o]
            w_zx = w_in[:, 0:D_INNER + CONV_DIM].astype(BF16)
            w_dt = jnp.pad(w_in[:, D_INNER + CONV_DIM:], ((0, 0), (0, 128 - SSD_HEADS))).astype(BF16)
            pad_h = lambda t: jnp.pad(t, ((0, 0), (0, 128 - SSD_HEADS))).reshape(2, 1, 128)
            z, xs, bm, cm, dt = _ssd_in(xa, row(norm1_g[li]), mod, w_zx, w_dt, ssd_conv_w[o],
                                        row(ssd_conv_b[o]), n_lat_tiles)
            y = _ssd_scan(xs, bm, cm, dt, pad_h(ssd_dt_bias[o]), pad_h(ssd_a_log[o]), t_len)
            d_skip = row(jnp.repeat(ssd_d[o], HEAD_DIM))
            if last:
                xa = _ssd_out(xa, mod, y, xs, z, d_skip, row(ssd_norm_g[o]), ssd_w_out[o].astype(BF16),
                              n_lat_tiles)
            else:
                raise NotImplementedError("an SSD layer that is not the last layer")
        xa = _mlp(xa, mod, row(norm2_g[li]), mlp_w1[li].astype(BF16), mlp_w2[li].astype(BF16),
                  row(norm_f_g), n_lat_tiles, last)
    return xa[:, 0:t_len]
```

```python
import functools
import math

import jax
import jax.numpy as jnp
from jax import lax
from jax.experimental import pallas as pl
from jax.experimental.pallas import tpu as pltpu

F32 = jnp.float32
BF16 = jnp.bfloat16

D_MODEL = 1024
FFN_DIM = 4 * D_MODEL
NORM_EPS = 1e-6
GRID_W = 64
ROPE_BASE = 10000.0

HEAD_DIM = 64
RWKV_DIM = 512
RWKV_HEADS = 8
RWKV_COLS = 1792
RWKV_GN_EPS = 64e-5
DIFF_HEADS = 4
DIFF_V = 128
DIFF_W = 512
HYB_IN = 3328

D_INNER = 2048
SSD_HEADS = 32
SSD_GROUPS = 8
SSD_HPG = 4
D_STATE = 128
CONV_W = 5
CONV_DIM = 4096
CHUNK = 128

TM = 256
HALO = 8
FFN_CHUNK = 512
WKV_TBLK = 128
VMEM_LIMIT = 56 * 1024 * 1024


def _cparams(sem):
    return pltpu.CompilerParams(dimension_semantics=sem, vmem_limit_bytes=VMEM_LIMIT)


def _const_spec(shape):
    nd = len(shape)
    return pl.BlockSpec(shape, lambda *_: (0,) * nd, pipeline_mode=pl.Buffered(1))


def _split2(x):
    hi = x.astype(BF16)
    lo = (x - hi.astype(F32)).astype(BF16)
    return hi, lo


def _segsum(x, ones):
    hi, lo = _split2(x)
    return (jnp.dot(hi, ones, preferred_element_type=F32)
            + jnp.dot(lo, ones, preferred_element_type=F32))


def _rms_mod(x, g, shift, scale):
    ms = jnp.mean(x * x, axis=-1, keepdims=True)
    return (x * lax.rsqrt(ms + NORM_EPS) * g) * (1.0 + scale) + shift


def _sigmoid(x):
    return 1.0 / (1.0 + jnp.exp(-x))


def _softplus(x):
    return jnp.maximum(x, 0.0) + jnp.log(1.0 + jnp.exp(-jnp.abs(x)))


def _silu(x):
    return x * _sigmoid(x)


def _ada_kernel(c_ref, w_ref, b_ref, o_ref):
    o_ref[0] = jnp.dot(_silu(c_ref[...]), w_ref[0], precision=lax.Precision.HIGHEST,
                       preferred_element_type=F32) + b_ref[0]


def _ada(cc, ada_w, ada_b):
    depth, d, n = ada_w.shape
    tn = 1536
    return pl.pallas_call(
        _ada_kernel,
        grid=(depth, n // tn),
        in_specs=[_const_spec(cc.shape),
                  pl.BlockSpec((1, d, tn), lambda l, j: (l, 0, j)),
                  pl.BlockSpec((1, 1, tn), lambda l, j: (l, 0, j))],
        out_specs=pl.BlockSpec((1, cc.shape[0], tn), lambda l, j: (l, 0, j)),
        out_shape=jax.ShapeDtypeStruct((depth, cc.shape[0], n), F32),
        compiler_params=_cparams(("parallel", "parallel")),
        name="ada",
    )(cc, ada_w, ada_b.reshape(depth, 1, n))


def _halo_specs(n_lat_tiles, width):
    per = TM // HALO
    last = (n_lat_tiles + 1) * per - 1
    prev = pl.BlockSpec((1, HALO, width), lambda b, i: (b, jnp.maximum(i * per - 1, 0), 0))
    cur = pl.BlockSpec((1, TM, width), lambda b, i: (b, i, 0))
    nxt = pl.BlockSpec((1, HALO, width), lambda b, i: (b, jnp.minimum((i + 1) * per, last), 0))
    return prev, cur, nxt


def _mod_spec(n_lat_tiles):
    return pl.BlockSpec((1, 6, D_MODEL), lambda b, i: (b * 2 + (i >= n_lat_tiles).astype(jnp.int32), 0, 0))


def _normed_with_halo(xp_ref, xc_ref, xn_ref, g_ref, mod_ref, hbuf, n_lat_tiles):
    i = pl.program_id(1)
    g = g_ref[...]
    shift, scale = mod_ref[0, 0:1, :], mod_ref[0, 1:2, :]
    prev_ok = jnp.logical_and(i >= 1, i < n_lat_tiles).astype(F32)
    next_ok = (i < n_lat_tiles - 1).astype(F32)
    hbuf[0:HALO, :] = _rms_mod(xp_ref[0], g, shift, scale) * prev_ok
    hbuf[HALO:HALO + TM, :] = _rms_mod(xc_ref[0], g, shift, scale)
    hbuf[HALO + TM:, :] = _rms_mod(xn_ref[0], g, shift, scale) * next_ok


def _hyb_in_kernel(xp_ref, xc_ref, xn_ref, g_ref, mod_ref, w_ref, mu_ref, kk_w_ref, ka_ref,
                   w0_ref, a0_ref, lora_ref, gup_ref, ones_ref, cos_ref, sin_ref,
                   r_out, v_out, a_out, dec_out, km_out, bb_out, g_out,
                   qr_out, qp_out, ks_out, vd_out, hbuf, pbuf, *, n_lat_tiles):
    _normed_with_halo(xp_ref, xc_ref, xn_ref, g_ref, mod_ref, hbuf, n_lat_tiles)
    pbuf[...] = jnp.dot(hbuf[...].astype(BF16), w_ref[...], preferred_element_type=F32)

    p = pbuf[HALO:HALO + TM, 0:RWKV_COLS]
    prev = pbuf[pl.ds(HALO - 1, TM), 0:RWKV_COLS]
    nxt = pbuf[pl.ds(HALO + 1, TM), 0:RWKV_COLS]
    p = p + (0.5 * (prev + nxt) - p) * mu_ref[...]
    r = p[:, 0:RWKV_DIM]
    k = p[:, RWKV_DIM:2 * RWKV_DIM]
    v = p[:, 2 * RWKV_DIM:3 * RWKV_DIM]
    lo_in = p[:, 3 * RWKV_DIM:3 * RWKV_DIM + 128]
    gd = p[:, 3 * RWKV_DIM + 128:RWKV_COLS]
    kkf = k * kk_w_ref[...]
    kk = kkf * lax.rsqrt(_segsum(kkf * kkf, ones_ref[...]) + 1e-12)
    r_out[0] = r
    v_out[0] = v
    a_out[0] = -kk
    g_out[0] = jnp.dot(_sigmoid(gd).astype(BF16), gup_ref[...], preferred_element_type=F32)
    lane = lax.broadcasted_iota(jnp.int32, lo_in.shape, 1)
    lo_act = jnp.where(lane < 64, jnp.tanh(lo_in), lo_in).astype(BF16)
    for d in range(2):
        up = jnp.dot(lo_act, lora_ref[d], preferred_element_type=F32)
        wlog = -_softplus(-(w0_ref[d:d + 1, :] + up[:, 0:RWKV_DIM])) - 0.5
        dec_out[d, 0] = jnp.exp(-jnp.exp(wlog))
        a = _sigmoid(a0_ref[d:d + 1, :] + up[:, RWKV_DIM:])
        km_out[d, 0] = k * (1.0 + (a - 1.0) * ka_ref[...])
        bb_out[d, 0] = kk * a

    o = RWKV_COLS
    vd_out[0] = pbuf[HALO:HALO + TM, o + 2 * DIFF_W:o + 3 * DIFF_W].astype(BF16)
    lane = lax.broadcasted_iota(jnp.int32, (TM, 128), 1)
    first = (lane % 32) < 16
    for s in range(DIFF_W // 128):
        ls = slice(s * 128, (s + 1) * 128)
        cos, sin = cos_ref[:, ls], sin_ref[:, ls]

        def rope(t):
            partner = jnp.where(first, pltpu.roll(t, 128 - 16, 1), pltpu.roll(t, 16, 1))
            return t * cos + partner * sin

        q = pbuf[HALO:HALO + TM, o + s * 128:o + (s + 1) * 128] * (HEAD_DIM ** -0.5)
        kd = pbuf[HALO:HALO + TM, o + DIFF_W + s * 128:o + DIFF_W + (s + 1) * 128]
        qp_out[0, :, ls] = q.astype(BF16)
        qr_out[0, :, ls] = rope(q).astype(BF16)
        ks_out[0, :, ls] = rope(kd).astype(BF16)


def _hyb_in(x, norm_g, mod, w_in, mu, k_k, k_a, w0, a0, lora, g_up, ones64, cos_t, sin_t, n_lat_tiles):
    bsz, n_tok, _ = x.shape
    n_tiles = n_tok // TM
    prev, cur, nxt = _halo_specs(n_lat_tiles, D_MODEL)
    tile = lambda w, dt: jax.ShapeDtypeStruct((bsz, n_tok, w), dt)
    tile2 = lambda w, dt: jax.ShapeDtypeStruct((2, bsz, n_tok, w), dt)
    o1 = pl.BlockSpec((1, TM, RWKV_DIM), lambda b, i: (b, i, 0))
    o2 = pl.BlockSpec((2, 1, TM, RWKV_DIM), lambda b, i: (0, b, i, 0))
    tab = pl.BlockSpec((TM, DIFF_W), lambda b, i: (i, 0))
    return pl.pallas_call(
        functools.partial(_hyb_in_kernel, n_lat_tiles=n_lat_tiles),
        grid=(bsz, n_tiles),
        in_specs=[prev, cur, nxt, _const_spec((1, D_MODEL)), _mod_spec(n_lat_tiles),
                  _const_spec(w_in.shape), _const_spec(mu.shape), _const_spec(k_k.shape),
                  _const_spec(k_a.shape), _const_spec(w0.shape), _const_spec(a0.shape),
                  _const_spec(lora.shape), _const_spec(g_up.shape), _const_spec(ones64.shape), tab, tab],
        out_specs=[o1, o1, o1, o2, o2, o2, o1, o1, o1, o1, o1],
        out_shape=[tile(RWKV_DIM, F32)] * 3 + [tile2(RWKV_DIM, F32)] * 3 + [tile(RWKV_DIM, F32)]
                  + [tile(DIFF_W, BF16)] * 4,
        scratch_shapes=[pltpu.VMEM((TM + 2 * HALO, D_MODEL), F32),
                        pltpu.VMEM((TM + 2 * HALO, HYB_IN), F32)],
        compiler_params=_cparams(("parallel", "parallel")),
        name="hyb_in",
    )(x, x, x, norm_g, mod, w_in, mu, k_k, k_a, w0, a0, lora, g_up, ones64, cos_t, sin_t)


def _wkv_kernel(r_ref, v_ref, a_ref, w_ref, k_ref, b_ref, y_ref, s_ref, *, gb, reverse):
    @pl.when(pl.program_id(1) == 0)
    def _():
        s_ref[...] = jnp.zeros_like(s_ref)

    lane = lax.broadcasted_iota(jnp.int32, (HEAD_DIM, 128), 1)
    sub = lax.broadcasted_iota(jnp.int32, (HEAD_DIM, 128), 0)
    sel = (lane % HEAD_DIM == sub).astype(F32)
    row = lax.broadcasted_iota(jnp.int32, (256, 128), 0)
    col = lax.broadcasted_iota(jnp.int32, (256, 128), 1)
    ones2 = ((row % 128) // HEAD_DIM == col // HEAD_DIM).astype(BF16)
    sub8 = lax.broadcasted_iota(jnp.int32, (8, 128), 0)
    n_grp = WKV_TBLK // 8
    n_pair = RWKV_DIM // 128
    chains = [(bb, p) for bb in range(gb) for p in range(n_pair)]
    n_ch = len(chains)

    def hilo(x):
        hi, lo = _split2(x)
        return jnp.concatenate([hi, lo], axis=1)

    def group(i, carry):
        base = pl.multiple_of((n_grp - 1 - i if reverse else i) * 8, 8)
        tiles = []
        for bb, p in chains:
            sl = slice(p * 128, (p + 1) * 128)
            tiles.append([ref[bb, pl.ds(base, 8), sl] for ref in (a_ref, v_ref, r_ref)]
                         + [ref[0, bb, pl.ds(base, 8), sl] for ref in (w_ref, k_ref, b_ref)])
        y_tiles = [jnp.zeros((8, 128), F32) for _ in chains]
        for tt in (reversed(range(8)) if reverse else range(8)):
            rows = [[t[tt:tt + 1, :] for t in tl] for tl in tiles]
            lhs = [hilo(s_ref[c] * rows[c][0]) for c in range(n_ch)]
            lhs += [hilo(sel * rows[c][1]) for c in range(n_ch)]
            red = jnp.dot(jnp.concatenate(lhs, axis=0), ones2, preferred_element_type=F32)
            zs = []
            for c in range(n_ch):
                _, _, r_t, w_t, k_t, b_t = rows[c]
                sa = red[c * HEAD_DIM:(c + 1) * HEAD_DIM]
                vb = red[(n_ch + c) * HEAD_DIM:(n_ch + c + 1) * HEAD_DIM]
                s = s_ref[c] * w_t + sa * b_t + vb * k_t
                s_ref[c] = s
                zs.append(hilo(s * r_t))
            yb = jnp.dot(jnp.concatenate(zs, axis=0), ones2, preferred_element_type=F32)
            for c in range(n_ch):
                y_row = jnp.sum(yb[c * HEAD_DIM:(c + 1) * HEAD_DIM] * sel, axis=0, keepdims=True)
                y_tiles[c] = jnp.where(sub8 == tt, y_row, y_tiles[c])
        for c, (bb, p) in enumerate(chains):
            y_ref[bb, pl.ds(base, 8), p * 128:(p + 1) * 128] = y_tiles[c]
        return carry

    lax.fori_loop(0, n_grp, group, 0)


def _wkv(r, v, a, dec, km, bb, n_lat_tok, reverse):
    bsz, n_tok, _ = r.shape
    gb = 4 if bsz % 4 == 0 else (2 if bsz % 2 == 0 else 1)
    n_blk = n_tok // WKV_TBLK
    n_lat = n_lat_tok // WKV_TBLK
    n_ctx = n_blk - n_lat
    d = 1 if reverse else 0

    def tb(j):
        return n_blk - 1 - j if reverse else jnp.where(j < n_ctx, n_lat + j, j - n_ctx)

    s1 = pl.BlockSpec((gb, WKV_TBLK, RWKV_DIM), lambda g, j: (g, tb(j), 0))
    s2 = pl.BlockSpec((1, gb, WKV_TBLK, RWKV_DIM), lambda g, j: (d, g, tb(j), 0))
    return pl.pallas_call(
        functools.partial(_wkv_kernel, gb=gb, reverse=reverse),
        grid=(bsz // gb, n_blk),
        in_specs=[s1, s1, s1, s2, s2, s2],
        out_specs=s1,
        out_shape=jax.ShapeDtypeStruct((bsz, n_tok, RWKV_DIM), F32),
        scratch_shapes=[pltpu.VMEM((gb * (RWKV_DIM // 128), HEAD_DIM, 128), F32)],
        compiler_params=_cparams(("parallel", "arbitrary")),
        name="wkv_bwd" if reverse else "wkv_fwd",
    )(r, v, a, dec, km, bb)


def _attn_kernel(lam_ref, qr_ref, qp_ref, kl_ref, kc_ref, vl_ref, vc_ref, o_ref, *, latent):
    lam = lam_ref[0]
    lane = lax.broadcasted_iota(jnp.int32, (1, 128), 1)
    nt = (((1,), (1,)), ((), ()))
    acc_l, acc_c = None, None
    for m in range(2):
        own = (lane // HEAD_DIM) == m
        qp = jnp.where(own, qp_ref[0], jnp.zeros_like(qp_ref[0]))
        sc = lax.dot_general(qp, kc_ref[0], nt, preferred_element_type=F32)
        mx = jnp.max(sc, axis=-1, keepdims=True)
        if latent:
            qr = jnp.where(own, qr_ref[0], jnp.zeros_like(qr_ref[0]))
            sl = lax.dot_general(qr, kl_ref[0], nt, preferred_element_type=F32)
            mx = jnp.maximum(mx, jnp.max(sl, axis=-1, keepdims=True))
            el = jnp.exp(sl - mx)
        ec = jnp.exp(sc - mx)
        den = jnp.sum(ec, axis=-1, keepdims=True)
        if latent:
            den = den + jnp.sum(el, axis=-1, keepdims=True)
        coef = (1.0 / den) * (1.0 if m == 0 else -lam)
        acc_c = ec * coef if acc_c is None else acc_c + ec * coef
        if latent:
            acc_l = el * coef if acc_l is None else acc_l + el * coef
    o = jnp.dot(acc_c.astype(BF16), vc_ref[0], preferred_element_type=F32)
    if latent:
        o = o + jnp.dot(acc_l.astype(BF16), vl_ref[0], preferred_element_type=F32)
    o_ref[0] = o


def _attn(lam, qr, qp, ks, vd, n_lat_tok, latent):
    bsz, n_tok, _ = qr.shape
    n_ctx_tok = n_tok - n_lat_tok
    ctx_blk = n_lat_tok // n_ctx_tok
    if latent:
        qb, n_q, q_off = TM, n_lat_tok // TM, 0
    else:
        qb, n_q, q_off = n_ctx_tok, 1, ctx_blk
    qspec = pl.BlockSpec((1, qb, 128), lambda b, h, i: (b, i + q_off, h))
    lat = pl.BlockSpec((1, n_lat_tok, 128), lambda b, h, i: (b, 0, h))
    ctx = pl.BlockSpec((1, n_ctx_tok, 128), lambda b, h, i: (b, ctx_blk, h))
    return pl.pallas_call(
        functools.partial(_attn_kernel, latent=latent),
        grid=(bsz, DIFF_HEADS, n_q),
        in_specs=[pl.BlockSpec(memory_space=pltpu.SMEM), qspec, qspec, lat, ctx, lat, ctx],
        out_specs=pl.BlockSpec((1, qb, 128), lambda b, h, i: (b, i, h)),
        out_shape=jax.ShapeDtypeStruct((bsz, n_q * qb, DIFF_W), F32),
        compiler_params=_cparams(("parallel", "parallel", "arbitrary")),
        name="attn_lat" if latent else "attn_ctx",
    )(lam, qr, qp, ks, ks, vd, vd)


def _hyb_out_kernel(x_ref, mod_ref, yf_ref, yb_ref, r_ref, v_ref, km_ref, g_ref, ol_ref, oc_ref, ones_ref,
                    rk_ref, lnw_ref, lnb_ref, sg_ref, wo_ref, out_ref, *, n_lat_tiles, sub_scale):
    i = pl.program_id(1)
    ones = ones_ref[...]
    y = yf_ref[0] + yb_ref[0]
    mean = _segsum(y, ones) * (1.0 / HEAD_DIM)
    yc = y - mean
    var = _segsum(yc * yc, ones) * (1.0 / HEAD_DIM)
    yn = yc * lax.rsqrt(var + RWKV_GN_EPS) * lnw_ref[...] + lnb_ref[...]
    r = r_ref[0]
    kb = 0.5 * (km_ref[0, 0] + km_ref[1, 0])
    bonus = _segsum(r * kb * rk_ref[...], ones) * v_ref[0]
    a_mix = ((yn + bonus) * g_ref[0]).astype(BF16)

    o = jnp.where(i < n_lat_tiles, ol_ref[0], oc_ref[0])
    subs = []
    for h in range(DIFF_HEADS):
        oh = o[:, h * DIFF_V:(h + 1) * DIFF_V]
        ms = jnp.mean(oh * oh, axis=-1, keepdims=True)
        subs.append(oh * lax.rsqrt(ms + NORM_EPS) * sg_ref[...] * sub_scale)
    sub = jnp.concatenate(subs, axis=1).astype(BF16)
    mix = (jnp.dot(a_mix, wo_ref[0:RWKV_DIM, :], preferred_element_type=F32)
           + jnp.dot(sub, wo_ref[RWKV_DIM:, :], preferred_element_type=F32))
    out_ref[0] = x_ref[0] + mod_ref[0, 2:3, :] * mix


def _hyb_out(x, mod, yf, yb, r, v, km, g, o_lat, o_ctx, ones64, r_k, ln_w, ln_b, subln_g, w_out,
             n_lat_tiles, sub_scale):
    bsz, n_tok, _ = x.shape
    n_tiles = n_tok // TM
    t1 = lambda w: pl.BlockSpec((1, TM, w), lambda b, i: (b, i, 0))
    t2 = pl.BlockSpec((2, 1, TM, RWKV_DIM), lambda b, i: (0, b, i, 0))
    ol = pl.BlockSpec((1, TM, DIFF_W), lambda b, i: (b, jnp.minimum(i, n_lat_tiles - 1), 0))
    oc = pl.BlockSpec((1, TM, DIFF_W), lambda b, i: (b, 0, 0))
    return pl.pallas_call(
        functools.partial(_hyb_out_kernel, n_lat_tiles=n_lat_tiles, sub_scale=sub_scale),
        grid=(bsz, n_tiles),
        in_specs=[t1(D_MODEL), _mod_spec(n_lat_tiles), t1(RWKV_DIM), t1(RWKV_DIM), t1(RWKV_DIM), t1(RWKV_DIM), t2,
                  t1(RWKV_DIM), ol, oc, _const_spec(ones64.shape), _const_spec(r_k.shape),
                  _const_spec(ln_w.shape), _const_spec(ln_b.shape), _const_spec(subln_g.shape),
                  _const_spec(w_out.shape)],
        out_specs=t1(D_MODEL),
        out_shape=jax.ShapeDtypeStruct(x.shape, F32),
        compiler_params=_cparams(("parallel", "parallel")),
        name="hyb_out",
    )(x, mod, yf, yb, r, v, km, g, o_lat, o_ctx, ones64, r_k, ln_w, ln_b, subln_g, w_out)


def _mlp_kernel(x_ref, mod_ref, g_ref, w1_ref, w2_ref, gf_ref, out_ref, *, final_norm):
    x = x_ref[0]
    h = _rms_mod(x, g_ref[...], mod_ref[0, 3:4, :], mod_ref[0, 4:5, :]).astype(BF16)
    acc = jnp.zeros((TM, D_MODEL), F32)
    for c in range(FFN_DIM // FFN_CHUNK):
        cs = slice(c * FFN_CHUNK, (c + 1) * FFN_CHUNK)
        u = jnp.maximum(jnp.dot(h, w1_ref[:, cs], preferred_element_type=F32), 0.0)
        acc = acc + jnp.dot((u * u).astype(BF16), w2_ref[cs, :], preferred_element_type=F32)
    out = x + mod_ref[0, 5:6, :] * acc
    if final_norm:
        ms = jnp.mean(out * out, axis=-1, keepdims=True)
        out = out * lax.rsqrt(ms + NORM_EPS) * gf_ref[...]
    out_ref[0] = out


def _mlp(x, mod, norm_g, w1, w2, norm_f, n_lat_tiles, final_norm):
    bsz, n_tok, _ = x.shape
    t1 = pl.BlockSpec((1, TM, D_MODEL), lambda b, i: (b, i, 0))
    return pl.pallas_call(
        functools.partial(_mlp_kernel, final_norm=final_norm),
        grid=(bsz, n_tok // TM),
        in_specs=[t1, _mod_spec(n_lat_tiles), _const_spec(norm_g.shape), _const_spec(w1.shape),
                  _const_spec(w2.shape), _const_spec(norm_f.shape)],
        out_specs=t1,
        out_shape=jax.ShapeDtypeStruct(x.shape, F32),
        compiler_params=_cparams(("parallel", "parallel")),
        name="mlp",
    )(x, mod, norm_g, w1, w2, norm_f)


def _ssd_in_kernel(xp_ref, xc_ref, xn_ref, g_ref, mod_ref, wzx_ref, wdt_ref, cw_ref, cb_ref,
                   z_out, xs_out, bm_out, cm_out, dt_out, hbuf, pbuf, *, n_lat_tiles):
    _normed_with_halo(xp_ref, xc_ref, xn_ref, g_ref, mod_ref, hbuf, n_lat_tiles)
    hc = hbuf[HALO:HALO + TM, :].astype(BF16)
    z_out[0] = jnp.dot(hc, wzx_ref[:, 0:D_INNER], preferred_element_type=F32)
    dt_out[0] = jnp.dot(hc, wdt_ref[...], preferred_element_type=F32)
    pbuf[...] = jnp.dot(hbuf[...].astype(BF16), wzx_ref[:, D_INNER:], preferred_element_type=F32)
    conv = cb_ref[...] + cw_ref[CONV_W // 2:CONV_W // 2 + 1, :] * pbuf[HALO:HALO + TM, :]
    for j in range(CONV_W):
        if j != CONV_W // 2:
            conv = conv + cw_ref[j:j + 1, :] * pbuf[pl.ds(HALO + j - CONV_W // 2, TM), :]
    act = _silu(conv)
    xs_out[0] = act[:, 0:D_INNER]
    bm_out[0] = act[:, D_INNER:D_INNER + SSD_GROUPS * D_STATE].astype(BF16)
    cm_out[0] = act[:, D_INNER + SSD_GROUPS * D_STATE:].astype(BF16)


def _ssd_in(x, norm_g, mod, w_zx, w_dt, conv_w, conv_b, n_lat_tiles):
    bsz, n_tok, _ = x.shape
    prev, cur, nxt = _halo_specs(n_lat_tiles, D_MODEL)
    t1 = lambda w: pl.BlockSpec((1, TM, w), lambda b, i: (b, i, 0))
    sh = lambda w, dt: jax.ShapeDtypeStruct((bsz, n_tok, w), dt)
    gw = SSD_GROUPS * D_STATE
    return pl.pallas_call(
        functools.partial(_ssd_in_kernel, n_lat_tiles=n_lat_tiles),
        grid=(bsz, n_tok // TM),
        in_specs=[prev, cur, nxt, _const_spec((1, D_MODEL)), _mod_spec(n_lat_tiles),
                  _const_spec(w_zx.shape), _const_spec(w_dt.shape), _const_spec(conv_w.shape),
                  _const_spec(conv_b.shape)],
        out_specs=[t1(D_INNER), t1(D_INNER), t1(gw), t1(gw), t1(128)],
        out_shape=[sh(D_INNER, F32), sh(D_INNER, F32), sh(gw, BF16), sh(gw, BF16), sh(128, F32)],
        scratch_shapes=[pltpu.VMEM((TM + 2 * HALO, D_MODEL), F32),
                        pltpu.VMEM((TM + 2 * HALO, CONV_DIM), F32)],
        compiler_params=_cparams(("parallel", "parallel")),
        name="ssd_in",
    )(x, x, x, norm_g, mod, w_zx, w_dt, conv_w, conv_b)


def _ssd_scan_kernel(xs_ref, bm_ref, cm_ref, dt_ref, dtb_ref, alog_ref, y_ref, h_ref):
    d = pl.program_id(0)
    j = pl.program_id(2)

    @pl.when(j == 0)
    def _():
        h_ref[...] = jnp.zeros_like(h_ref)

    row = lax.broadcasted_iota(jnp.int32, (CHUNK, CHUNK), 0)
    col = lax.broadcasted_iota(jnp.int32, (CHUNK, CHUNK), 1)
    keep = jnp.where(d == 0, row - col, col - row) >= 0
    dtd = _softplus(dt_ref[0] + dtb_ref[0])
    dta = dtd * (-jnp.exp(alog_ref[0]))
    cs = jnp.dot(keep.astype(F32), dta, precision=lax.Precision.HIGHEST, preferred_element_type=F32)
    cs_t = cs.T
    tot = jnp.where(d == 0, cs[CHUNK - 1:CHUNK, :], cs[0:1, :])
    e_cs = jnp.exp(cs)
    e_end = jnp.exp(tot - cs)
    e_tot = jnp.exp(tot)
    nt = (((1,), (1,)), ((), ()))
    for g in range(SSD_GROUPS):
        bc = bm_ref[0, :, g * D_STATE:(g + 1) * D_STATE]
        cc = cm_ref[0, :, g * D_STATE:(g + 1) * D_STATE]
        cb = lax.dot_general(cc, bc, nt, preferred_element_type=F32)
        for hh in range(SSD_HPG):
            h = g * SSD_HPG + hh
            hs = slice(h * HEAD_DIM, (h + 1) * HEAD_DIM)
            seg = cs[:, h:h + 1] - cs_t[h:h + 1, :]
            lmat = jnp.exp(jnp.where(keep, seg, -jnp.inf))
            xdt = xs_ref[0, :, hs] * dtd[:, h:h + 1]
            state = h_ref[h]
            y = jnp.dot((cb * lmat).astype(BF16), xdt.astype(BF16), preferred_element_type=F32)
            y = y + lax.dot_general(cc, state.astype(BF16), nt, preferred_element_type=F32) * e_cs[:, h:h + 1]
            y_ref[0, 0, :, hs] = y
            xw_t = (xdt * e_end[:, h:h + 1]).T.astype(BF16)
            h_ref[h] = state * e_tot[:, h:h + 1] + jnp.dot(xw_t, bc, preferred_element_type=F32)


def _ssd_scan(xs, bm, cm, dt, dt_bias, a_log, n_lat_tok):
    bsz, n_tok, _ = xs.shape
    n_blk = n_tok // CHUNK
    n_lat = n_lat_tok // CHUNK
    n_ctx = n_blk - n_lat

    def tb(d, j):
        fwd = jnp.where(j < n_ctx, n_lat + j, j - n_ctx)
        return jnp.where(d == 0, fwd, n_blk - 1 - j)

    t1 = lambda w: pl.BlockSpec((1, CHUNK, w), lambda d, b, j: (b, tb(d, j), 0))
    par = pl.BlockSpec((1, 1, 128), lambda d, b, j: (d, 0, 0))
    return pl.pallas_call(
        _ssd_scan_kernel,
        grid=(2, bsz, n_blk),
        in_specs=[t1(D_INNER), t1(SSD_GROUPS * D_STATE), t1(SSD_GROUPS * D_STATE), t1(128), par, par],
        out_specs=pl.BlockSpec((1, 1, CHUNK, D_INNER), lambda d, b, j: (d, b, tb(d, j), 0)),
        out_shape=jax.ShapeDtypeStruct((2, bsz, n_tok, D_INNER), F32),
        scratch_shapes=[pltpu.VMEM((SSD_HEADS, HEAD_DIM, D_STATE), F32)],
        compiler_params=_cparams(("parallel", "parallel", "arbitrary")),
        name="ssd_scan",
    )(xs, bm, cm, dt, dt_bias, a_log)


def _ssd_out_kernel(x_ref, mod_ref, y_ref, xs_ref, z_ref, dsk_ref, ng_ref, wo_ref, out_ref):
    y = y_ref[0, 0] + y_ref[1, 0] + dsk_ref[...] * xs_ref[0]
    yg = y * _silu(z_ref[0])
    gw = D_INNER // SSD_GROUPS
    parts = []
    for g in range(SSD_GROUPS):
        t = yg[:, g * gw:(g + 1) * gw]
        ms = jnp.mean(t * t, axis=-1, keepdims=True)
        parts.append((t * lax.rsqrt(ms + NORM_EPS) * ng_ref[:, g * gw:(g + 1) * gw]).astype(BF16))
    yn = jnp.concatenate(parts, axis=1)
    out_ref[0] = x_ref[0] + mod_ref[0, 2:3, :] * jnp.dot(yn, wo_ref[...], preferred_element_type=F32)


def _ssd_out(x, mod, y, xs, z, d_skip, norm_g, w_out, n_lat_tiles):
    bsz = x.shape[0]
    t1 = lambda w: pl.BlockSpec((1, TM, w), lambda b, i: (b, i, 0))
    return pl.pallas_call(
        _ssd_out_kernel,
        grid=(bsz, n_lat_tiles),
        in_specs=[t1(D_MODEL), _mod_spec(n_lat_tiles),
                  pl.BlockSpec((2, 1, TM, D_INNER), lambda b, i: (0, b, i, 0)),
                  t1(D_INNER), t1(D_INNER), _const_spec(d_skip.shape), _const_spec(norm_g.shape),
                  _const_spec(w_out.shape)],
        out_specs=t1(D_MODEL),
        out_shape=jax.ShapeDtypeStruct((bsz, n_lat_tiles * TM, D_MODEL), F32),
        compiler_params=_cparams(("parallel", "parallel")),
        name="ssd_out",
    )(x, mod, y, xs, z, d_skip, norm_g, w_out)


def _rope_tables(t_len, n_ctx_tok):
    n_rows = t_len // GRID_W
    rows = jnp.broadcast_to(jnp.arange(n_rows)[:, None], (n_rows, GRID_W)).reshape(-1)
    cols = jnp.broadcast_to(jnp.arange(GRID_W)[None, :], (n_rows, GRID_W)).reshape(-1)
    inv = ROPE_BASE ** (-jnp.arange(0, HEAD_DIM // 2, 2, dtype=F32) / (HEAD_DIM // 2))
    ang_r = rows.astype(F32)[:, None] * inv
    ang_c = cols.astype(F32)[:, None] * inv
    cos = jnp.concatenate([jnp.cos(ang_r)] * 2 + [jnp.cos(ang_c)] * 2, axis=1)
    sin = jnp.concatenate([-jnp.sin(ang_r), jnp.sin(ang_r), -jnp.sin(ang_c), jnp.sin(ang_c)], axis=1)
    reps = DIFF_W // HEAD_DIM
    cos = jnp.concatenate([jnp.tile(cos, (1, reps)), jnp.ones((n_ctx_tok, DIFF_W), F32)], axis=0)
    sin = jnp.concatenate([jnp.tile(sin, (1, reps)), jnp.zeros((n_ctx_tok, DIFF_W), F32)], axis=0)
    return cos, sin


def kernel(x, c, ctx, c_ctx, ada_w, ada_b, norm1_g, norm2_g, mlp_w1, mlp_w2, hy_w_in, hy_w_out, rwkv_mu, rwkv_w0, rwkv_w_up, rwkv_a0, rwkv_a_up, rwkv_g_up, rwkv_k_k, rwkv_k_a, rwkv_r_k, rwkv_ln_w, rwkv_ln_b, diff_lq1, diff_lk1, diff_lq2, diff_lk2, diff_subln_g, ssd_w_in, ssd_conv_w, ssd_conv_b, ssd_dt_bias, ssd_a_log, ssd_d, ssd_norm_g, ssd_w_out, norm_f_g):
    bsz, t_len, _ = x.shape
    l_ctx = ctx.shape[1]
    assert l_ctx == TM and t_len % TM == 0 and t_len % l_ctx == 0
    depth = ada_w.shape[0]
    n_lat_tiles = t_len // TM
    row = lambda t: t.reshape(1, -1)

    cc = jnp.zeros((16, D_MODEL), F32).at[0:bsz].set(c).at[bsz].set(c_ctx)
    mods = _ada(cc, ada_w, ada_b)
    mod_lat = mods[:, 0:bsz]
    mod_ctx = jnp.broadcast_to(mods[:, bsz:bsz + 1], mod_lat.shape)
    mods = jnp.stack([mod_lat, mod_ctx], axis=2).reshape(depth, bsz * 2, 6, D_MODEL)

    xa = jnp.concatenate([x, ctx], axis=1)
    ones64 = (jnp.arange(RWKV_DIM)[:, None] // HEAD_DIM == jnp.arange(RWKV_DIM)[None, :] // HEAD_DIM).astype(BF16)
    cos_t, sin_t = _rope_tables(t_len, l_ctx)

    for li in range(depth):
        last = li == depth - 1
        mod = mods[li]
        if li % 2 == 0:
            e = li // 2
            lam_init = 0.8 - 0.6 * math.exp(-0.3 * li)
            lam = (jnp.exp(jnp.sum(diff_lq1[e] * diff_lk1[e])) - jnp.exp(jnp.sum(diff_lq2[e] * diff_lk2[e]))
                   + lam_init).reshape(1).astype(F32)
            zero = jnp.zeros((64, RWKV_DIM), F32)
            lora = jnp.stack([jnp.concatenate(
                [jnp.concatenate([rwkv_w_up[e, d], zero], axis=1),
                 jnp.concatenate([zero, rwkv_a_up[e, d]], axis=1)], axis=0) for d in range(2)]).astype(BF16)
            (r, v, a, dec, km, bb, g, qr, qp, ks, vd) = _hyb_in(
                xa, row(norm1_g[li]), mod, hy_w_in[e].astype(BF16), row(rwkv_mu[e]), row(rwkv_k_k[e]),
                row(rwkv_k_a[e]), rwkv_w0[e], rwkv_a0[e], lora, rwkv_g_up[e].astype(BF16), ones64,
                cos_t, sin_t, n_lat_tiles)
            yf = _wkv(r, v, a, dec, km, bb, t_len, False)
            yb = _wkv(r, v, a, dec, km, bb, t_len, True)
            o_lat = _attn(lam, qr, qp, ks, vd, t_len, True)
            o_ctx = _attn(lam, qr, qp, ks, vd, t_len, False)
            xa = _hyb_out(xa, mod, yf, yb, r, v, km, g, o_lat, o_ctx, ones64, row(rwkv_r_k[e]),
                          row(rwkv_ln_w[e]), row(rwkv_ln_b[e]),
                          row(diff_subln_g[e]), hy_w_out[e].astype(BF16), n_lat_tiles, 1.0 - lam_init)
        else:
            o = li // 2
            w_in = ssd_w_in[o]
            w_zx = w_in[:, 0:D_INNER + CONV_DIM].astype(BF16)
            w_dt = jnp.pad(w_in[:, D_INNER + CONV_DIM:], ((0, 0), (0, 128 - SSD_HEADS))).astype(BF16)
            pad_h = lambda t: jnp.pad(t, ((0, 0), (0, 128 - SSD_HEADS))).reshape(2, 1, 128)
            z, xs, bm, cm, dt = _ssd_in(xa, row(norm1_g[li]), mod, w_zx, w_dt, ssd_conv_w[o],
                                        row(ssd_conv_b[o]), n_lat_tiles)
            y = _ssd_scan(xs, bm, cm, dt, pad_h(ssd_dt_bias[o]), pad_h(ssd_a_log[o]), t_len)
            d_skip = row(jnp.repeat(ssd_d[o], HEAD_DIM))
            if last:
                xa = _ssd_out(xa, mod, y, xs, z, d_skip, row(ssd_norm_g[o]), ssd_w_out[o].astype(BF16),
                              n_lat_tiles)
            else:
                raise NotImplementedError("an SSD layer that is not the last layer")
        xa = _mlp(xa, mod, row(norm2_g[li]), mlp_w1[li].astype(BF16), mlp_w2[li].astype(BF16),
                  row(norm_f_g), n_lat_tiles, last)
    return xa[:, 0:t_len]
```

```python
import functools
import math

import jax
import jax.numpy as jnp
from jax import lax
from jax.experimental import pallas as pl
from jax.experimental.pallas import tpu as pltpu

F32 = jnp.float32
BF16 = jnp.bfloat16

D_MODEL = 1024
FFN_DIM = 4 * D_MODEL
NORM_EPS = 1e-6
GRID_W = 64
ROPE_BASE = 10000.0

HEAD_DIM = 64
RWKV_DIM = 512
RWKV_HEADS = 8
RWKV_COLS = 1792
RWKV_GN_EPS = 64e-5
DIFF_HEADS = 4
DIFF_V = 128
DIFF_W = 512
HYB_IN = 3328

D_INNER = 2048
SSD_HEADS = 32
SSD_GROUPS = 8
SSD_HPG = 4
D_STATE = 128
CONV_W = 5
CONV_DIM = 4096
CHUNK = 128

TM = 256
HALO = 8
FFN_CHUNK = 512
WKV_TBLK = 128
WKV_SUB = 8
VMEM_LIMIT = 56 * 1024 * 1024


def _cparams(sem):
    return pltpu.CompilerParams(dimension_semantics=sem, vmem_limit_bytes=VMEM_LIMIT)


def _const_spec(shape):
    nd = len(shape)
    return pl.BlockSpec(shape, lambda *_: (0,) * nd, pipeline_mode=pl.Buffered(1))


def _split2(x):
    hi = x.astype(BF16)
    lo = (x - hi.astype(F32)).astype(BF16)
    return hi, lo


def _segsum(x, ones):
    hi, lo = _split2(x)
    return (jnp.dot(hi, ones, preferred_element_type=F32)
            + jnp.dot(lo, ones, preferred_element_type=F32))


def _rms_mod(x, g, shift, scale):
    ms = jnp.mean(x * x, axis=-1, keepdims=True)
    return (x * lax.rsqrt(ms + NORM_EPS) * g) * (1.0 + scale) + shift


def _sigmoid(x):
    return 1.0 / (1.0 + jnp.exp(-x))


def _softplus(x):
    return jnp.maximum(x, 0.0) + jnp.log(1.0 + jnp.exp(-jnp.abs(x)))


def _silu(x):
    return x * _sigmoid(x)


def _ada_kernel(c_ref, w_ref, b_ref, o_ref):
    o_ref[0] = jnp.dot(_silu(c_ref[...]), w_ref[0], precision=lax.Precision.HIGHEST,
                       preferred_element_type=F32) + b_ref[0]


def _ada(cc, ada_w, ada_b):
    depth, d, n = ada_w.shape
    tn = 1536
    return pl.pallas_call(
        _ada_kernel,
        grid=(depth, n // tn),
        in_specs=[_const_spec(cc.shape),
                  pl.BlockSpec((1, d, tn), lambda l, j: (l, 0, j)),
                  pl.BlockSpec((1, 1, tn), lambda l, j: (l, 0, j))],
        out_specs=pl.BlockSpec((1, cc.shape[0], tn), lambda l, j: (l, 0, j)),
        out_shape=jax.ShapeDtypeStruct((depth, cc.shape[0], n), F32),
        compiler_params=_cparams(("parallel", "parallel")),
        name="ada",
    )(cc, ada_w, ada_b.reshape(depth, 1, n))


def _halo_specs(n_lat_tiles, width):
    per = TM // HALO
    last = (n_lat_tiles + 1) * per - 1
    prev = pl.BlockSpec((1, HALO, width), lambda b, i: (b, jnp.maximum(i * per - 1, 0), 0))
    cur = pl.BlockSpec((1, TM, width), lambda b, i: (b, i, 0))
    nxt = pl.BlockSpec((1, HALO, width), lambda b, i: (b, jnp.minimum((i + 1) * per, last), 0))
    return prev, cur, nxt


def _mod_spec(n_lat_tiles):
    return pl.BlockSpec((1, 6, D_MODEL), lambda b, i: (b * 2 + (i >= n_lat_tiles).astype(jnp.int32), 0, 0))


def _normed_with_halo(xp_ref, xc_ref, xn_ref, g_ref, mod_ref, hbuf, n_lat_tiles):
    i = pl.program_id(1)
    g = g_ref[...]
    shift, scale = mod_ref[0, 0:1, :], mod_ref[0, 1:2, :]
    prev_ok = jnp.logical_and(i >= 1, i < n_lat_tiles).astype(F32)
    next_ok = (i < n_lat_tiles - 1).astype(F32)
    hbuf[0:HALO, :] = _rms_mod(xp_ref[0], g, shift, scale) * prev_ok
    hbuf[HALO:HALO + TM, :] = _rms_mod(xc_ref[0], g, shift, scale)
    hbuf[HALO + TM:, :] = _rms_mod(xn_ref[0], g, shift, scale) * next_ok


def _hyb_in_kernel(xp_ref, xc_ref, xn_ref, g_ref, mod_ref, w_ref, mu_ref, kk_w_ref, ka_ref,
                   w0_ref, a0_ref, lora_ref, gup_ref, ones_ref, cos_ref, sin_ref,
                   r_out, v_out, a_out, dec_out, km_out, bb_out, g_out,
                   qr_out, qp_out, ks_out, vd_out, hbuf, pbuf, *, n_lat_tiles):
    _normed_with_halo(xp_ref, xc_ref, xn_ref, g_ref, mod_ref, hbuf, n_lat_tiles)
    pbuf[...] = jnp.dot(hbuf[...].astype(BF16), w_ref[...], preferred_element_type=F32)

    p = pbuf[HALO:HALO + TM, 0:RWKV_COLS]
    prev = pbuf[pl.ds(HALO - 1, TM), 0:RWKV_COLS]
    nxt = pbuf[pl.ds(HALO + 1, TM), 0:RWKV_COLS]
    p = p + (0.5 * (prev + nxt) - p) * mu_ref[...]
    r = p[:, 0:RWKV_DIM]
    k = p[:, RWKV_DIM:2 * RWKV_DIM]
    v = p[:, 2 * RWKV_DIM:3 * RWKV_DIM]
    lo_in = p[:, 3 * RWKV_DIM:3 * RWKV_DIM + 128]
    gd = p[:, 3 * RWKV_DIM + 128:RWKV_COLS]
    kkf = k * kk_w_ref[...]
    kk = kkf * lax.rsqrt(_segsum(kkf * kkf, ones_ref[...]) + 1e-12)
    r_out[0] = r
    v_out[0] = v
    a_out[0] = -kk
    g_out[0] = jnp.dot(_sigmoid(gd).astype(BF16), gup_ref[...], preferred_element_type=F32)
    lane = lax.broadcasted_iota(jnp.int32, lo_in.shape, 1)
    lo_act = jnp.where(lane < 64, jnp.tanh(lo_in), lo_in).astype(BF16)
    for d in range(2):
        up = jnp.dot(lo_act, lora_ref[d], preferred_element_type=F32)
        wlog = -_softplus(-(w0_ref[d:d + 1, :] + up[:, 0:RWKV_DIM])) - 0.5
        dec_out[d, 0] = jnp.exp(-jnp.exp(wlog))
        a = _sigmoid(a0_ref[d:d + 1, :] + up[:, RWKV_DIM:])
        km_out[d, 0] = k * (1.0 + (a - 1.0) * ka_ref[...])
        bb_out[d, 0] = kk * a

    o = RWKV_COLS
    vd_out[0] = pbuf[HALO:HALO + TM, o + 2 * DIFF_W:o + 3 * DIFF_W].astype(BF16)
    lane = lax.broadcasted_iota(jnp.int32, (TM, 128), 1)
    first = (lane % 32) < 16
    for s in range(DIFF_W // 128):
        ls = slice(s * 128, (s + 1) * 128)
        cos, sin = cos_ref[:, ls], sin_ref[:, ls]

        def rope(t):
            partner = jnp.where(first, pltpu.roll(t, 128 - 16, 1), pltpu.roll(t, 16, 1))
            return t * cos + partner * sin

        q = pbuf[HALO:HALO + TM, o + s * 128:o + (s + 1) * 128] * (HEAD_DIM ** -0.5)
        kd = pbuf[HALO:HALO + TM, o + DIFF_W + s * 128:o + DIFF_W + (s + 1) * 128]
        qp_out[0, :, ls] = q.astype(BF16)
        qr_out[0, :, ls] = rope(q).astype(BF16)
        ks_out[0, :, ls] = rope(kd).astype(BF16)


def _hyb_in(x, norm_g, mod, w_in, mu, k_k, k_a, w0, a0, lora, g_up, ones64, cos_t, sin_t, n_lat_tiles):
    bsz, n_tok, _ = x.shape
    n_tiles = n_tok // TM
    prev, cur, nxt = _halo_specs(n_lat_tiles, D_MODEL)
    tile = lambda w, dt: jax.ShapeDtypeStruct((bsz, n_tok, w), dt)
    tile2 = lambda w, dt: jax.ShapeDtypeStruct((2, bsz, n_tok, w), dt)
    o1 = pl.BlockSpec((1, TM, RWKV_DIM), lambda b, i: (b, i, 0))
    o2 = pl.BlockSpec((2, 1, TM, RWKV_DIM), lambda b, i: (0, b, i, 0))
    tab = pl.BlockSpec((TM, DIFF_W), lambda b, i: (i, 0))
    return pl.pallas_call(
        functools.partial(_hyb_in_kernel, n_lat_tiles=n_lat_tiles),
        grid=(bsz, n_tiles),
        in_specs=[prev, cur, nxt, _const_spec((1, D_MODEL)), _mod_spec(n_lat_tiles),
                  _const_spec(w_in.shape), _const_spec(mu.shape), _const_spec(k_k.shape),
                  _const_spec(k_a.shape), _const_spec(w0.shape), _const_spec(a0.shape),
                  _const_spec(lora.shape), _const_spec(g_up.shape), _const_spec(ones64.shape), tab, tab],
        out_specs=[o1, o1, o1, o2, o2, o2, o1, o1, o1, o1, o1],
        out_shape=[tile(RWKV_DIM, F32)] * 3 + [tile2(RWKV_DIM, F32)] * 3 + [tile(RWKV_DIM, F32)]
                  + [tile(DIFF_W, BF16)] * 4,
        scratch_shapes=[pltpu.VMEM((TM + 2 * HALO, D_MODEL), F32),
                        pltpu.VMEM((TM + 2 * HALO, HYB_IN), F32)],
        compiler_params=_cparams(("parallel", "parallel")),
        name="hyb_in",
    )(x, x, x, norm_g, mod, w_in, mu, k_k, k_a, w0, a0, lora, g_up, ones64, cos_t, sin_t)


def _wkv_kernel(r_ref, v_ref, a_ref, w_ref, k_ref, b_ref, y_ref, s_ref, *, gb, reverse):
    @pl.when(pl.program_id(1) == 0)
    def _():
        s_ref[...] = jnp.zeros_like(s_ref)

    lane = lax.broadcasted_iota(jnp.int32, (HEAD_DIM, 128), 1)
    sub = lax.broadcasted_iota(jnp.int32, (HEAD_DIM, 128), 0)
    sel = (lane % HEAD_DIM == sub).astype(F32)
    row = lax.broadcasted_iota(jnp.int32, (256, 128), 0)
    col = lax.broadcasted_iota(jnp.int32, (256, 128), 1)
    ones2 = ((row % 128) // HEAD_DIM == col // HEAD_DIM).astype(BF16)
    sub8 = lax.broadcasted_iota(jnp.int32, (8, 128), 0)
    sel_b = sel.astype(BF16)
    n_grp = WKV_TBLK // 8
    n_pair = RWKV_DIM // 128
    chains = [(bb, p) for bb in range(gb) for p in range(n_pair)]
    n_ch = len(chains)
    n_sub = min(n_ch, WKV_SUB)

    def hilo(x):
        hi, lo = _split2(x)
        return jnp.concatenate([hi, lo], axis=1)

    def spread(v_t):
        hi, lo = _split2(v_t)
        return jnp.concatenate([sel_b * hi, sel_b * lo], axis=1)

    def group(i, carry):
        base = pl.multiple_of((n_grp - 1 - i if reverse else i) * 8, 8)
        tiles = []
        for bb, p in chains:
            sl = slice(p * 128, (p + 1) * 128)
            tiles.append([ref[bb, pl.ds(base, 8), sl] for ref in (a_ref, v_ref, r_ref)]
                         + [ref[0, bb, pl.ds(base, 8), sl] for ref in (w_ref, k_ref, b_ref)])
        y_tiles = [jnp.zeros((8, 128), F32) for _ in chains]
        st = [s_ref[c] for c in range(n_ch)]
        dots = jnp.dot(jnp.concatenate([hilo(tl[5] * tl[2]) for tl in tiles]
                                       + [hilo(tl[4] * tl[2]) for tl in tiles], axis=0),
                       ones2, preferred_element_type=F32)
        wr = [tl[3] * tl[2] + tl[0] * dots[c * 8:(c + 1) * 8] for c, tl in enumerate(tiles)]
        kr = [dots[(n_ch + c) * 8:(n_ch + c + 1) * 8] for c in range(n_ch)]
        for tt in (reversed(range(8)) if reverse else range(8)):
            rows = [[t[tt:tt + 1, :] for t in tl] for tl in tiles]
            for sub_grp in range(0, n_ch, n_sub):
                cs = list(range(sub_grp, sub_grp + n_sub))
                lhs = ([hilo(st[c] * rows[c][0]) for c in cs] + [spread(rows[c][1]) for c in cs]
                       + [hilo(st[c] * wr[c][tt:tt + 1, :]) for c in cs])
                red = jnp.dot(jnp.concatenate(lhs, axis=0), ones2, preferred_element_type=F32)
                for i, c in enumerate(cs):
                    _, v_t, _, w_t, k_t, b_t = rows[c]
                    sa = red[i * HEAD_DIM:(i + 1) * HEAD_DIM]
                    vb = red[(n_sub + i) * HEAD_DIM:(n_sub + i + 1) * HEAD_DIM]
                    rz = red[(2 * n_sub + i) * HEAD_DIM:(2 * n_sub + i + 1) * HEAD_DIM]
                    st[c] = st[c] * w_t + sa * b_t + vb * k_t
                    y_row = jnp.sum(rz * sel, axis=0, keepdims=True) + v_t * kr[c][tt:tt + 1, :]
                    y_tiles[c] = jnp.where(sub8 == tt, y_row, y_tiles[c])
        for c, (bb, p) in enumerate(chains):
            y_ref[bb, pl.ds(base, 8), p * 128:(p + 1) * 128] = y_tiles[c]
            s_ref[c] = st[c]
        return carry

    lax.fori_loop(0, n_grp, group, 0)


def _wkv(r, v, a, dec, km, bb, n_lat_tok, reverse):
    bsz, n_tok, _ = r.shape
    gb = 4 if bsz % 4 == 0 else (2 if bsz % 2 == 0 else 1)
    n_blk = n_tok // WKV_TBLK
    n_lat = n_lat_tok // WKV_TBLK
    n_ctx = n_blk - n_lat
    d = 1 if reverse else 0

    def tb(j):
        return n_blk - 1 - j if reverse else jnp.where(j < n_ctx, n_lat + j, j - n_ctx)

    s1 = pl.BlockSpec((gb, WKV_TBLK, RWKV_DIM), lambda g, j: (g, tb(j), 0))
    s2 = pl.BlockSpec((1, gb, WKV_TBLK, RWKV_DIM), lambda g, j: (d, g, tb(j), 0))
    return pl.pallas_call(
        functools.partial(_wkv_kernel, gb=gb, reverse=reverse),
        grid=(bsz // gb, n_blk),
        in_specs=[s1, s1, s1, s2, s2, s2],
        out_specs=s1,
        out_shape=jax.ShapeDtypeStruct((bsz, n_tok, RWKV_DIM), F32),
        scratch_shapes=[pltpu.VMEM((gb * (RWKV_DIM // 128), HEAD_DIM, 128), F32)],
        compiler_params=_cparams(("parallel", "arbitrary")),
        name="wkv_bwd" if reverse else "wkv_fwd",
    )(r, v, a, dec, km, bb)


def _attn_kernel(lam_ref, qr_ref, qp_ref, kl_ref, kc_ref, vl_ref, vc_ref, o_ref, *, latent):
    lam = lam_ref[0]
    lane = lax.broadcasted_iota(jnp.int32, (1, 128), 1)
    nt = (((1,), (1,)), ((), ()))
    acc_l, acc_c = None, None
    for m in range(2):
        own = (lane // HEAD_DIM) == m
        qp = jnp.where(own, qp_ref[0], jnp.zeros_like(qp_ref[0]))
        sc = lax.dot_general(qp, kc_ref[0], nt, preferred_element_type=F32)
        mx = jnp.max(sc, axis=-1, keepdims=True)
        if latent:
            qr = jnp.where(own, qr_ref[0], jnp.zeros_like(qr_ref[0]))
            sl = lax.dot_general(qr, kl_ref[0], nt, preferred_element_type=F32)
            mx = jnp.maximum(mx, jnp.max(sl, axis=-1, keepdims=True))
            el = jnp.exp(sl - mx)
        ec = jnp.exp(sc - mx)
        den = jnp.sum(ec, axis=-1, keepdims=True)
        if latent:
            den = den + jnp.sum(el, axis=-1, keepdims=True)
        coef = (1.0 / den) * (1.0 if m == 0 else -lam)
        acc_c = ec * coef if acc_c is None else acc_c + ec * coef
        if latent:
            acc_l = el * coef if acc_l is None else acc_l + el * coef
    o = jnp.dot(acc_c.astype(BF16), vc_ref[0], preferred_element_type=F32)
    if latent:
        o = o + jnp.dot(acc_l.astype(BF16), vl_ref[0], preferred_element_type=F32)
    o_ref[0] = o


def _attn(lam, qr, qp, ks, vd, n_lat_tok, latent):
    bsz, n_tok, _ = qr.shape
    n_ctx_tok = n_tok - n_lat_tok
    ctx_blk = n_lat_tok // n_ctx_tok
    if latent:
        qb, n_q, q_off = TM, n_lat_tok // TM, 0
    else:
        qb, n_q, q_off = n_ctx_tok, 1, ctx_blk
    qspec = pl.BlockSpec((1, qb, 128), lambda b, h, i: (b, i + q_off, h))
    lat = pl.BlockSpec((1, n_lat_tok, 128), lambda b, h, i: (b, 0, h))
    ctx = pl.BlockSpec((1, n_ctx_tok, 128), lambda b, h, i: (b, ctx_blk, h))
    return pl.pallas_call(
        functools.partial(_attn_kernel, latent=latent),
        grid=(bsz, DIFF_HEADS, n_q),
        in_specs=[pl.BlockSpec(memory_space=pltpu.SMEM), qspec, qspec, lat, ctx, lat, ctx],
        out_specs=pl.BlockSpec((1, qb, 128), lambda b, h, i: (b, i, h)),
        out_shape=jax.ShapeDtypeStruct((bsz, n_q * qb, DIFF_W), F32),
        compiler_params=_cparams(("parallel", "parallel", "arbitrary")),
        name="attn_lat" if latent else "attn_ctx",
    )(lam, qr, qp, ks, ks, vd, vd)


def _hyb_out_kernel(x_ref, mod_ref, yf_ref, yb_ref, r_ref, v_ref, km_ref, g_ref, ol_ref, oc_ref, ones_ref,
                    rk_ref, lnw_ref, lnb_ref, sg_ref, wo_ref, out_ref, *, n_lat_tiles, sub_scale):
    i = pl.program_id(1)
    ones = ones_ref[...]
    y = yf_ref[0] + yb_ref[0]
    mean = _segsum(y, ones) * (1.0 / HEAD_DIM)
    yc = y - mean
    var = _segsum(yc * yc, ones) * (1.0 / HEAD_DIM)
    yn = yc * lax.rsqrt(var + RWKV_GN_EPS) * lnw_ref[...] + lnb_ref[...]
    r = r_ref[0]
    kb = 0.5 * (km_ref[0, 0] + km_ref[1, 0])
    bonus = _segsum(r * kb * rk_ref[...], ones) * v_ref[0]
    a_mix = ((yn + bonus) * g_ref[0]).astype(BF16)

    o = jnp.where(i < n_lat_tiles, ol_ref[0], oc_ref[0])
    subs = []
    for h in range(DIFF_HEADS):
        oh = o[:, h * DIFF_V:(h + 1) * DIFF_V]
        ms = jnp.mean(oh * oh, axis=-1, keepdims=True)
        subs.append(oh * lax.rsqrt(ms + NORM_EPS) * sg_ref[...] * sub_scale)
    sub = jnp.concatenate(subs, axis=1).astype(BF16)
    mix = (jnp.dot(a_mix, wo_ref[0:RWKV_DIM, :], preferred_element_type=F32)
           + jnp.dot(sub, wo_ref[RWKV_DIM:, :], preferred_element_type=F32))
    out_ref[0] = x_ref[0] + mod_ref[0, 2:3, :] * mix


def _hyb_out(x, mod, yf, yb, r, v, km, g, o_lat, o_ctx, ones64, r_k, ln_w, ln_b, subln_g, w_out,
             n_lat_tiles, sub_scale):
    bsz, n_tok, _ = x.shape
    n_tiles = n_tok // TM
    t1 = lambda w: pl.BlockSpec((1, TM, w), lambda b, i: (b, i, 0))
    t2 = pl.BlockSpec((2, 1, TM, RWKV_DIM), lambda b, i: (0, b, i, 0))
    ol = pl.BlockSpec((1, TM, DIFF_W), lambda b, i: (b, jnp.minimum(i, n_lat_tiles - 1), 0))
    oc = pl.BlockSpec((1, TM, DIFF_W), lambda b, i: (b, 0, 0))
    return pl.pallas_call(
        functools.partial(_hyb_out_kernel, n_lat_tiles=n_lat_tiles, sub_scale=sub_scale),
        grid=(bsz, n_tiles),
        in_specs=[t1(D_MODEL), _mod_spec(n_lat_tiles), t1(RWKV_DIM), t1(RWKV_DIM), t1(RWKV_DIM), t1(RWKV_DIM), t2,
                  t1(RWKV_DIM), ol, oc, _const_spec(ones64.shape), _const_spec(r_k.shape),
                  _const_spec(ln_w.shape), _const_spec(ln_b.shape), _const_spec(subln_g.shape),
                  _const_spec(w_out.shape)],
        out_specs=t1(D_MODEL),
        out_shape=jax.ShapeDtypeStruct(x.shape, F32),
        compiler_params=_cparams(("parallel", "parallel")),
        name="hyb_out",
    )(x, mod, yf, yb, r, v, km, g, o_lat, o_ctx, ones64, r_k, ln_w, ln_b, subln_g, w_out)


def _mlp_kernel(x_ref, mod_ref, g_ref, w1_ref, w2_ref, gf_ref, out_ref, *, final_norm):
    x = x_ref[0]
    h = _rms_mod(x, g_ref[...], mod_ref[0, 3:4, :], mod_ref[0, 4:5, :]).astype(BF16)
    acc = jnp.zeros((TM, D_MODEL), F32)
    for c in range(FFN_DIM // FFN_CHUNK):
        cs = slice(c * FFN_CHUNK, (c + 1) * FFN_CHUNK)
        u = jnp.maximum(jnp.dot(h, w1_ref[:, cs], preferred_element_type=F32), 0.0)
        acc = acc + jnp.dot((u * u).astype(BF16), w2_ref[cs, :], preferred_element_type=F32)
    out = x + mod_ref[0, 5:6, :] * acc
    if final_norm:
        ms = jnp.mean(out * out, axis=-1, keepdims=True)
        out = out * lax.rsqrt(ms + NORM_EPS) * gf_ref[...]
    out_ref[0] = out


def _mlp(x, mod, norm_g, w1, w2, norm_f, n_lat_tiles, final_norm):
    bsz, n_tok, _ = x.shape
    t1 = pl.BlockSpec((1, TM, D_MODEL), lambda b, i: (b, i, 0))
    return pl.pallas_call(
        functools.partial(_mlp_kernel, final_norm=final_norm),
        grid=(bsz, n_tok // TM),
        in_specs=[t1, _mod_spec(n_lat_tiles), _const_spec(norm_g.shape), _const_spec(w1.shape),
                  _const_spec(w2.shape), _const_spec(norm_f.shape)],
        out_specs=t1,
        out_shape=jax.ShapeDtypeStruct(x.shape, F32),
        compiler_params=_cparams(("parallel", "parallel")),
        name="mlp",
    )(x, mod, norm_g, w1, w2, norm_f)


def _ssd_in_kernel(xp_ref, xc_ref, xn_ref, g_ref, mod_ref, wzx_ref, wdt_ref, cw_ref, cb_ref,
                   z_out, xs_out, xt_out, bm_out, cm_out, dt_out, hbuf, pbuf, *, n_lat_tiles):
    _normed_with_halo(xp_ref, xc_ref, xn_ref, g_ref, mod_ref, hbuf, n_lat_tiles)
    hc = hbuf[HALO:HALO + TM, :].astype(BF16)
    z_out[0] = jnp.dot(hc, wzx_ref[:, 0:D_INNER], preferred_element_type=F32)
    dt_out[0] = jnp.dot(hc, wdt_ref[...], preferred_element_type=F32)
    pbuf[...] = jnp.dot(hbuf[...].astype(BF16), wzx_ref[:, D_INNER:], preferred_element_type=F32)
    conv = cb_ref[...] + cw_ref[CONV_W // 2:CONV_W // 2 + 1, :] * pbuf[HALO:HALO + TM, :]
    for j in range(CONV_W):
        if j != CONV_W // 2:
            conv = conv + cw_ref[j:j + 1, :] * pbuf[pl.ds(HALO + j - CONV_W // 2, TM), :]
    act = _silu(conv)
    xs_out[0] = act[:, 0:D_INNER]
    for c in range(D_INNER // 128):
        xt_out[0, c * 128:(c + 1) * 128, :] = act[:, c * 128:(c + 1) * 128].T
    bm_out[0] = act[:, D_INNER:D_INNER + SSD_GROUPS * D_STATE].astype(BF16)
    cm_out[0] = act[:, D_INNER + SSD_GROUPS * D_STATE:].astype(BF16)


def _ssd_in(x, norm_g, mod, w_zx, w_dt, conv_w, conv_b, n_lat_tiles):
    bsz, n_tok, _ = x.shape
    prev, cur, nxt = _halo_specs(n_lat_tiles, D_MODEL)
    t1 = lambda w: pl.BlockSpec((1, TM, w), lambda b, i: (b, i, 0))
    sh = lambda w, dt: jax.ShapeDtypeStruct((bsz, n_tok, w), dt)
    gw = SSD_GROUPS * D_STATE
    return pl.pallas_call(
        functools.partial(_ssd_in_kernel, n_lat_tiles=n_lat_tiles),
        grid=(bsz, n_tok // TM),
        in_specs=[prev, cur, nxt, _const_spec((1, D_MODEL)), _mod_spec(n_lat_tiles),
                  _const_spec(w_zx.shape), _const_spec(w_dt.shape), _const_spec(conv_w.shape),
                  _const_spec(conv_b.shape)],
        out_specs=[t1(D_INNER), t1(D_INNER), pl.BlockSpec((1, D_INNER, TM), lambda b, i: (b, 0, i)),
                   t1(gw), t1(gw), t1(128)],
        out_shape=[sh(D_INNER, F32), sh(D_INNER, F32), jax.ShapeDtypeStruct((bsz, D_INNER, n_tok), F32),
                   sh(gw, BF16), sh(gw, BF16), sh(128, F32)],
        scratch_shapes=[pltpu.VMEM((TM + 2 * HALO, D_MODEL), F32),
                        pltpu.VMEM((TM + 2 * HALO, CONV_DIM), F32)],
        compiler_params=_cparams(("parallel", "parallel")),
        name="ssd_in",
    )(x, x, x, norm_g, mod, w_zx, w_dt, conv_w, conv_b)


def _ssd_scan_kernel(xt_ref, bm_ref, cm_ref, dt_ref, dtb_ref, alog_ref, y_ref, h_ref):
    d = pl.program_id(0)
    j = pl.program_id(2)

    @pl.when(j == 0)
    def _():
        h_ref[...] = jnp.zeros_like(h_ref)

    row = lax.broadcasted_iota(jnp.int32, (CHUNK, CHUNK), 0)
    col = lax.broadcasted_iota(jnp.int32, (CHUNK, CHUNK), 1)
    keep = jnp.where(d == 0, row - col, col - row) >= 0
    dtd = _softplus(dt_ref[0] + dtb_ref[0])
    dta = dtd * (-jnp.exp(alog_ref[0]))
    cs = jnp.dot(keep.astype(F32), dta, precision=lax.Precision.HIGHEST, preferred_element_type=F32)
    cs_t = cs.T
    dtd_t = dtd.T
    tot_c = jnp.where(d == 0, cs_t[:, CHUNK - 1:CHUNK], cs_t[:, 0:1])
    coef_t = dtd_t * jnp.exp(tot_c - cs_t)
    e_tot = jnp.exp(jnp.broadcast_to(tot_c, (128, D_STATE)))
    nt = (((1,), (1,)), ((), ()))
    zero = jnp.zeros((HEAD_DIM, 2 * CHUNK), BF16)

    def rows2(t, q):
        return jnp.concatenate([jnp.broadcast_to(t[2 * q + i:2 * q + i + 1, :], (HEAD_DIM, t.shape[1]))
                                for i in range(2)], axis=0)

    for g in range(SSD_GROUPS):
        bc = bm_ref[0, :, g * D_STATE:(g + 1) * D_STATE]
        cc = cm_ref[0, :, g * D_STATE:(g + 1) * D_STATE]
        cb = lax.dot_general(cc, bc, nt, preferred_element_type=F32)
        cc_f = cc.astype(F32)
        for qq in range(SSD_HPG // 2):
            q = g * (SSD_HPG // 2) + qq
            xt = xt_ref[0, q * 128:(q + 1) * 128, :]
            xdt_t = (xt * rows2(dtd_t, q)).astype(BF16)
            xw_t = (xt * rows2(coef_t, q)).astype(BF16)
            state = h_ref[q]
            st_b = state.astype(BF16)
            lhs = []
            for i in range(2):
                h = 2 * q + i
                cs_col = jnp.broadcast_to(cs[:, h:h + 1], (CHUNK, CHUNK))
                lmat = jnp.exp(jnp.where(keep, cs_col - cs_t[h:h + 1, :], -jnp.inf))
                lhs += [(cb * lmat).astype(BF16), (cc_f * jnp.exp(cs_col)).astype(BF16)]
            w_top = jnp.concatenate([xdt_t[0:HEAD_DIM], st_b[0:HEAD_DIM], zero], axis=1)
            w_bot = jnp.concatenate([zero, xdt_t[HEAD_DIM:], st_b[HEAD_DIM:]], axis=1)
            y_ref[0, 0, :, q * 128:(q + 1) * 128] = lax.dot_general(
                jnp.concatenate(lhs, axis=1), jnp.concatenate([w_top, w_bot], axis=0), nt,
                preferred_element_type=F32)
            h_ref[q] = state * rows2(e_tot, q) + jnp.dot(xw_t, bc, preferred_element_type=F32)


def _ssd_scan(xs_t, bm, cm, dt, dt_bias, a_log, n_lat_tok):
    bsz, _, n_tok = xs_t.shape
    n_blk = n_tok // CHUNK
    n_lat = n_lat_tok // CHUNK
    n_ctx = n_blk - n_lat

    def tb(d, j):
        fwd = jnp.where(j < n_ctx, n_lat + j, j - n_ctx)
        return jnp.where(d == 0, fwd, n_blk - 1 - j)

    t1 = lambda w: pl.BlockSpec((1, CHUNK, w), lambda d, b, j: (b, tb(d, j), 0))
    par = pl.BlockSpec((1, 1, 128), lambda d, b, j: (d, 0, 0))
    return pl.pallas_call(
        _ssd_scan_kernel,
        grid=(2, bsz, n_blk),
        in_specs=[pl.BlockSpec((1, D_INNER, CHUNK), lambda d, b, j: (b, 0, tb(d, j))),
                  t1(SSD_GROUPS * D_STATE), t1(SSD_GROUPS * D_STATE), t1(128), par, par],
        out_specs=pl.BlockSpec((1, 1, CHUNK, D_INNER), lambda d, b, j: (d, b, tb(d, j), 0)),
        out_shape=jax.ShapeDtypeStruct((2, bsz, n_tok, D_INNER), F32),
        scratch_shapes=[pltpu.VMEM((SSD_HEADS // 2, 2 * HEAD_DIM, D_STATE), F32)],
        compiler_params=_cparams(("parallel", "parallel", "arbitrary")),
        name="ssd_scan",
    )(xs_t, bm, cm, dt, dt_bias, a_log)


def _ssd_out_kernel(x_ref, mod_ref, y_ref, xs_ref, z_ref, dsk_ref, ng_ref, wo_ref, out_ref):
    y = y_ref[0, 0] + y_ref[1, 0] + dsk_ref[...] * xs_ref[0]
    yg = y * _silu(z_ref[0])
    gw = D_INNER // SSD_GROUPS
    parts = []
    for g in range(SSD_GROUPS):
        t = yg[:, g * gw:(g + 1) * gw]
        ms = jnp.mean(t * t, axis=-1, keepdims=True)
        parts.append((t * lax.rsqrt(ms + NORM_EPS) * ng_ref[:, g * gw:(g + 1) * gw]).astype(BF16))
    yn = jnp.concatenate(parts, axis=1)
    out_ref[0] = x_ref[0] + mod_ref[0, 2:3, :] * jnp.dot(yn, wo_ref[...], preferred_element_type=F32)


def _ssd_out(x, mod, y, xs, z, d_skip, norm_g, w_out, n_lat_tiles):
    bsz = x.shape[0]
    t1 = lambda w: pl.BlockSpec((1, TM, w), lambda b, i: (b, i, 0))
    return pl.pallas_call(
        _ssd_out_kernel,
        grid=(bsz, n_lat_tiles),
        in_specs=[t1(D_MODEL), _mod_spec(n_lat_tiles),
                  pl.BlockSpec((2, 1, TM, D_INNER), lambda b, i: (0, b, i, 0)),
                  t1(D_INNER), t1(D_INNER), _const_spec(d_skip.shape), _const_spec(norm_g.shape),
                  _const_spec(w_out.shape)],
        out_specs=t1(D_MODEL),
        out_shape=jax.ShapeDtypeStruct((bsz, n_lat_tiles * TM, D_MODEL), F32),
        compiler_params=_cparams(("parallel", "parallel")),
        name="ssd_out",
    )(x, mod, y, xs, z, d_skip, norm_g, w_out)


def _rope_tables(t_len, n_ctx_tok):
    n_rows = t_len // GRID_W
    rows = jnp.broadcast_to(jnp.arange(n_rows)[:, None], (n_rows, GRID_W)).reshape(-1)
    cols = jnp.broadcast_to(jnp.arange(GRID_W)[None, :], (n_rows, GRID_W)).reshape(-1)
    inv = ROPE_BASE ** (-jnp.arange(0, HEAD_DIM // 2, 2, dtype=F32) / (HEAD_DIM // 2))
    ang_r = rows.astype(F32)[:, None] * inv
    ang_c = cols.astype(F32)[:, None] * inv
    cos = jnp.concatenate([jnp.cos(ang_r)] * 2 + [jnp.cos(ang_c)] * 2, axis=1)
    sin = jnp.concatenate([-jnp.sin(ang_r), jnp.sin(ang_r), -jnp.sin(ang_c), jnp.sin(ang_c)], axis=1)
    reps = DIFF_W // HEAD_DIM
    cos = jnp.concatenate([jnp.tile(cos, (1, reps)), jnp.ones((n_ctx_tok, DIFF_W), F32)], axis=0)
    sin = jnp.concatenate([jnp.tile(sin, (1, reps)), jnp.zeros((n_ctx_tok, DIFF_W), F32)], axis=0)
    return cos, sin


def kernel(x, c, ctx, c_ctx, ada_w, ada_b, norm1_g, norm2_g, mlp_w1, mlp_w2, hy_w_in, hy_w_out, rwkv_mu, rwkv_w0, rwkv_w_up, rwkv_a0, rwkv_a_up, rwkv_g_up, rwkv_k_k, rwkv_k_a, rwkv_r_k, rwkv_ln_w, rwkv_ln_b, diff_lq1, diff_lk1, diff_lq2, diff_lk2, diff_subln_g, ssd_w_in, ssd_conv_w, ssd_conv_b, ssd_dt_bias, ssd_a_log, ssd_d, ssd_norm_g, ssd_w_out, norm_f_g):
    bsz, t_len, _ = x.shape
    l_ctx = ctx.shape[1]
    assert l_ctx == TM and t_len % TM == 0 and t_len % l_ctx == 0
    depth = ada_w.shape[0]
    n_lat_tiles = t_len // TM
    row = lambda t: t.reshape(1, -1)

    cc = jnp.zeros((16, D_MODEL), F32).at[0:bsz].set(c).at[bsz].set(c_ctx)
    mods = _ada(cc, ada_w, ada_b)
    mod_lat = mods[:, 0:bsz]
    mod_ctx = jnp.broadcast_to(mods[:, bsz:bsz + 1], mod_lat.shape)
    mods = jnp.stack([mod_lat, mod_ctx], axis=2).reshape(depth, bsz * 2, 6, D_MODEL)

    xa = jnp.concatenate([x, ctx], axis=1)
    ones64 = (jnp.arange(RWKV_DIM)[:, None] // HEAD_DIM == jnp.arange(RWKV_DIM)[None, :] // HEAD_DIM).astype(BF16)
    cos_t, sin_t = _rope_tables(t_len, l_ctx)

    for li in range(depth):
        last = li == depth - 1
        mod = mods[li]
        if li % 2 == 0:
            e = li // 2
            lam_init = 0.8 - 0.6 * math.exp(-0.3 * li)
            lam = (jnp.exp(jnp.sum(diff_lq1[e] * diff_lk1[e])) - jnp.exp(jnp.sum(diff_lq2[e] * diff_lk2[e]))
                   + lam_init).reshape(1).astype(F32)
            zero = jnp.zeros((64, RWKV_DIM), F32)
            lora = jnp.stack([jnp.concatenate(
                [jnp.concatenate([rwkv_w_up[e, d], zero], axis=1),
                 jnp.concatenate([zero, rwkv_a_up[e, d]], axis=1)], axis=0) for d in range(2)]).astype(BF16)
            (r, v, a, dec, km, bb, g, qr, qp, ks, vd) = _hyb_in(
                xa, row(norm1_g[li]), mod, hy_w_in[e].astype(BF16), row(rwkv_mu[e]), row(rwkv_k_k[e]),
                row(rwkv_k_a[e]), rwkv_w0[e], rwkv_a0[e], lora, rwkv_g_up[e].astype(BF16), ones64,
                cos_t, sin_t, n_lat_tiles)
            yf = _wkv(r, v, a, dec, km, bb, t_len, False)
            yb = _wkv(r, v, a, dec, km, bb, t_len, True)
            o_lat = _attn(lam, qr, qp, ks, vd, t_len, True)
            o_ctx = _attn(lam, qr, qp, ks, vd, t_len, False)
            xa = _hyb_out(xa, mod, yf, yb, r, v, km, g, o_lat, o_ctx, ones64, row(rwkv_r_k[e]),
                          row(rwkv_ln_w[e]), row(rwkv_ln_b[e]),
                          row(diff_subln_g[e]), hy_w_out[e].astype(BF16), n_lat_tiles, 1.0 - lam_init)
        else:
            o = li // 2
            w_in = ssd_w_in[o]
            w_zx = w_in[:, 0:D_INNER + CONV_DIM].astype(BF16)
            w_dt = jnp.pad(w_in[:, D_INNER + CONV_DIM:], ((0, 0), (0, 128 - SSD_HEADS))).astype(BF16)
            pad_h = lambda t: jnp.pad(t, ((0, 0), (0, 128 - SSD_HEADS))).reshape(2, 1, 128)
            z, xs, xs_t, bm, cm, dt = _ssd_in(xa, row(norm1_g[li]), mod, w_zx, w_dt, ssd_conv_w[o],
                                              row(ssd_conv_b[o]), n_lat_tiles)
            y = _ssd_scan(xs_t, bm, cm, dt, pad_h(ssd_dt_bias[o]), pad_h(ssd_a_log[o]), t_len)
            d_skip = row(jnp.repeat(ssd_d[o], HEAD_DIM))
            if last:
                xa = _ssd_out(xa, mod, y, xs, z, d_skip, row(ssd_norm_g[o]), ssd_w_out[o].astype(BF16),
                              n_lat_tiles)
            else:
                raise NotImplementedError("an SSD layer that is not the last layer")
        xa = _mlp(xa, mod, row(norm2_g[li]), mlp_w1[li].astype(BF16), mlp_w2[li].astype(BF16),
                  row(norm_f_g), n_lat_tiles, last)
    return xa[:, 0:t_len]
```

```python
import functools
import math

import jax
import jax.numpy as jnp
from jax import lax
from jax.experimental import pallas as pl
from jax.experimental.pallas import tpu as pltpu

F32 = jnp.float32
BF16 = jnp.bfloat16

D_MODEL = 1024
FFN_DIM = 4 * D_MODEL
NORM_EPS = 1e-6
GRID_W = 64
ROPE_BASE = 10000.0

HEAD_DIM = 64
RWKV_DIM = 512
RWKV_HEADS = 8
RWKV_COLS = 1792
RWKV_GN_EPS = 64e-5
DIFF_HEADS = 4
DIFF_V = 128
DIFF_W = 512
HYB_IN = 3328

D_INNER = 2048
SSD_HEADS = 32
SSD_GROUPS = 8
SSD_HPG = 4
D_STATE = 128
CONV_W = 5
CONV_DIM = 4096
CHUNK = 128

TM = 256
HALO = 8
FFN_CHUNK = 512
WKV_TBLK = 128
ATTN_KEY_SEG = 1024
WKV_SUB = 8
VMEM_LIMIT = 56 * 1024 * 1024


def _cparams(sem):
    return pltpu.CompilerParams(dimension_semantics=sem, vmem_limit_bytes=VMEM_LIMIT)


def _const_spec(shape):
    nd = len(shape)
    return pl.BlockSpec(shape, lambda *_: (0,) * nd, pipeline_mode=pl.Buffered(1))


def _split2(x):
    hi = x.astype(BF16)
    lo = (x - hi.astype(F32)).astype(BF16)
    return hi, lo


def _segsum(x, ones):
    hi, lo = _split2(x)
    return (jnp.dot(hi, ones, preferred_element_type=F32)
            + jnp.dot(lo, ones, preferred_element_type=F32))


def _rms_mod(x, g, shift, scale):
    ms = jnp.mean(x * x, axis=-1, keepdims=True)
    return (x * lax.rsqrt(ms + NORM_EPS) * g) * (1.0 + scale) + shift


def _sigmoid(x):
    return 1.0 / (1.0 + jnp.exp(-x))


def _softplus(x):
    return jnp.maximum(x, 0.0) + jnp.log(1.0 + jnp.exp(-jnp.abs(x)))


def _silu(x):
    return x * _sigmoid(x)


def _ada_kernel(c_ref, w_ref, b_ref, o_ref):
    o_ref[0] = jnp.dot(_silu(c_ref[...]), w_ref[0], precision=lax.Precision.HIGHEST,
                       preferred_element_type=F32) + b_ref[0]


def _ada(cc, ada_w, ada_b):
    depth, d, n = ada_w.shape
    tn = 1536
    return pl.pallas_call(
        _ada_kernel,
        grid=(depth, n // tn),
        in_specs=[_const_spec(cc.shape),
                  pl.BlockSpec((1, d, tn), lambda l, j: (l, 0, j)),
                  pl.BlockSpec((1, 1, tn), lambda l, j: (l, 0, j))],
        out_specs=pl.BlockSpec((1, cc.shape[0], tn), lambda l, j: (l, 0, j)),
        out_shape=jax.ShapeDtypeStruct((depth, cc.shape[0], n), F32),
        compiler_params=_cparams(("parallel", "parallel")),
        name="ada",
    )(cc, ada_w, ada_b.reshape(depth, 1, n))


def _halo_specs(n_lat_tiles, width):
    per = TM // HALO
    last = (n_lat_tiles + 1) * per - 1
    prev = pl.BlockSpec((1, HALO, width), lambda b, i: (b, jnp.maximum(i * per - 1, 0), 0))
    cur = pl.BlockSpec((1, TM, width), lambda b, i: (b, i, 0))
    nxt = pl.BlockSpec((1, HALO, width), lambda b, i: (b, jnp.minimum((i + 1) * per, last), 0))
    return prev, cur, nxt


def _mod_spec(n_lat_tiles):
    return pl.BlockSpec((1, 6, D_MODEL), lambda b, i: (b * 2 + (i >= n_lat_tiles).astype(jnp.int32), 0, 0))


def _normed_with_halo(xp_ref, xc_ref, xn_ref, g_ref, mod_ref, hbuf, n_lat_tiles):
    i = pl.program_id(1)
    g = g_ref[...]
    shift, scale = mod_ref[0, 0:1, :], mod_ref[0, 1:2, :]
    prev_ok = jnp.logical_and(i >= 1, i < n_lat_tiles).astype(F32)
    next_ok = (i < n_lat_tiles - 1).astype(F32)
    hbuf[0:HALO, :] = _rms_mod(xp_ref[0], g, shift, scale) * prev_ok
    hbuf[HALO:HALO + TM, :] = _rms_mod(xc_ref[0], g, shift, scale)
    hbuf[HALO + TM:, :] = _rms_mod(xn_ref[0], g, shift, scale) * next_ok


def _hyb_in_kernel(xp_ref, xc_ref, xn_ref, g_ref, mod_ref, w_ref, mu_ref, kk_w_ref, ka_ref,
                   w0_ref, a0_ref, lora_ref, gup_ref, ones_ref, cos_ref, sin_ref,
                   r_out, v_out, a_out, dec_out, km_out, bb_out, g_out,
                   qr_out, qp_out, ks_out, vd_out, hbuf, pbuf, *, n_lat_tiles):
    _normed_with_halo(xp_ref, xc_ref, xn_ref, g_ref, mod_ref, hbuf, n_lat_tiles)
    pbuf[...] = jnp.dot(hbuf[...].astype(BF16), w_ref[...], preferred_element_type=F32)

    p = pbuf[HALO:HALO + TM, 0:RWKV_COLS]
    prev = pbuf[pl.ds(HALO - 1, TM), 0:RWKV_COLS]
    nxt = pbuf[pl.ds(HALO + 1, TM), 0:RWKV_COLS]
    p = p + (0.5 * (prev + nxt) - p) * mu_ref[...]
    r = p[:, 0:RWKV_DIM]
    k = p[:, RWKV_DIM:2 * RWKV_DIM]
    v = p[:, 2 * RWKV_DIM:3 * RWKV_DIM]
    lo_in = p[:, 3 * RWKV_DIM:3 * RWKV_DIM + 128]
    gd = p[:, 3 * RWKV_DIM + 128:RWKV_COLS]
    kkf = k * kk_w_ref[...]
    kk = kkf * lax.rsqrt(_segsum(kkf * kkf, ones_ref[...]) + 1e-12)
    r_out[0] = r
    v_out[0] = v
    a_out[0] = -kk
    g_out[0] = jnp.dot(_sigmoid(gd).astype(BF16), gup_ref[...], preferred_element_type=F32)
    lane = lax.broadcasted_iota(jnp.int32, lo_in.shape, 1)
    lo_act = jnp.where(lane < 64, jnp.tanh(lo_in), lo_in).astype(BF16)
    for d in range(2):
        up = jnp.dot(lo_act, lora_ref[d], preferred_element_type=F32)
        wlog = -_softplus(-(w0_ref[d:d + 1, :] + up[:, 0:RWKV_DIM])) - 0.5
        dec_out[d, 0] = jnp.exp(-jnp.exp(wlog))
        a = _sigmoid(a0_ref[d:d + 1, :] + up[:, RWKV_DIM:])
        km_out[d, 0] = k * (1.0 + (a - 1.0) * ka_ref[...])
        bb_out[d, 0] = kk * a

    o = RWKV_COLS
    vd_out[0] = pbuf[HALO:HALO + TM, o + 2 * DIFF_W:o + 3 * DIFF_W].astype(BF16)
    lane = lax.broadcasted_iota(jnp.int32, (TM, 128), 1)
    first = (lane % 32) < 16
    for s in range(DIFF_W // 128):
        ls = slice(s * 128, (s + 1) * 128)
        cos, sin = cos_ref[:, ls], sin_ref[:, ls]

        def rope(t):
            partner = jnp.where(first, pltpu.roll(t, 128 - 16, 1), pltpu.roll(t, 16, 1))
            return t * cos + partner * sin

        q = pbuf[HALO:HALO + TM, o + s * 128:o + (s + 1) * 128] * (HEAD_DIM ** -0.5)
        kd = pbuf[HALO:HALO + TM, o + DIFF_W + s * 128:o + DIFF_W + (s + 1) * 128]
        qp_out[0, :, ls] = q.astype(BF16)
        qr_out[0, :, ls] = rope(q).astype(BF16)
        ks_out[0, :, ls] = rope(kd).astype(BF16)


def _hyb_in(x, norm_g, mod, w_in, mu, k_k, k_a, w0, a0, lora, g_up, ones64, cos_t, sin_t, n_lat_tiles):
    bsz, n_tok, _ = x.shape
    n_tiles = n_tok // TM
    prev, cur, nxt = _halo_specs(n_lat_tiles, D_MODEL)
    tile = lambda w, dt: jax.ShapeDtypeStruct((bsz, n_tok, w), dt)
    tile2 = lambda w, dt: jax.ShapeDtypeStruct((2, bsz, n_tok, w), dt)
    o1 = pl.BlockSpec((1, TM, RWKV_DIM), lambda b, i: (b, i, 0))
    o2 = pl.BlockSpec((2, 1, TM, RWKV_DIM), lambda b, i: (0, b, i, 0))
    tab = pl.BlockSpec((TM, DIFF_W), lambda b, i: (i, 0))
    return pl.pallas_call(
        functools.partial(_hyb_in_kernel, n_lat_tiles=n_lat_tiles),
        grid=(bsz, n_tiles),
        in_specs=[prev, cur, nxt, _const_spec((1, D_MODEL)), _mod_spec(n_lat_tiles),
                  _const_spec(w_in.shape), _const_spec(mu.shape), _const_spec(k_k.shape),
                  _const_spec(k_a.shape), _const_spec(w0.shape), _const_spec(a0.shape),
                  _const_spec(lora.shape), _const_spec(g_up.shape), _const_spec(ones64.shape), tab, tab],
        out_specs=[o1, o1, o1, o2, o2, o2, o1, o1, o1, o1, o1],
        out_shape=[tile(RWKV_DIM, F32)] * 3 + [tile2(RWKV_DIM, F32)] * 3 + [tile(RWKV_DIM, F32)]
                  + [tile(DIFF_W, BF16)] * 4,
        scratch_shapes=[pltpu.VMEM((TM + 2 * HALO, D_MODEL), F32),
                        pltpu.VMEM((TM + 2 * HALO, HYB_IN), F32)],
        compiler_params=_cparams(("parallel", "parallel")),
        name="hyb_in",
    )(x, x, x, norm_g, mod, w_in, mu, k_k, k_a, w0, a0, lora, g_up, ones64, cos_t, sin_t)


def _wkv_kernel(r_ref, v_ref, a_ref, w_ref, k_ref, b_ref, y_ref, s_ref, *, gb, reverse):
    @pl.when(pl.program_id(1) == 0)
    def _():
        s_ref[...] = jnp.zeros_like(s_ref)

    lane = lax.broadcasted_iota(jnp.int32, (HEAD_DIM, 128), 1)
    sub = lax.broadcasted_iota(jnp.int32, (HEAD_DIM, 128), 0)
    sel = (lane % HEAD_DIM == sub).astype(F32)
    row = lax.broadcasted_iota(jnp.int32, (256, 128), 0)
    col = lax.broadcasted_iota(jnp.int32, (256, 128), 1)
    ones2 = ((row % 128) // HEAD_DIM == col // HEAD_DIM).astype(BF16)
    sub8 = lax.broadcasted_iota(jnp.int32, (8, 128), 0)
    sel_b = sel.astype(BF16)
    row = lax.broadcasted_iota(jnp.int32, (256, 256), 0)
    col = lax.broadcasted_iota(jnp.int32, (256, 256), 1)
    ones_bd = (row // HEAD_DIM == col // HEAD_DIM).astype(BF16)
    n_grp = WKV_TBLK // 8
    n_pair = RWKV_DIM // 128
    chains = [(bb, p) for bb in range(gb) for p in range(n_pair)]
    n_ch = len(chains)
    n_sub = min(n_ch, WKV_SUB)
    half = n_sub // 2

    def hilo(x):
        hi, lo = _split2(x)
        return jnp.concatenate([hi, lo], axis=1)

    def pairs(xs):
        return [jnp.concatenate([xs[i], xs[i + half]], axis=1) for i in range(half)]

    def group(i, carry):
        base = pl.multiple_of((n_grp - 1 - i if reverse else i) * 8, 8)
        tiles = []
        for bb, p in chains:
            sl = slice(p * 128, (p + 1) * 128)
            tiles.append([ref[bb, pl.ds(base, 8), sl] for ref in (a_ref, v_ref, r_ref)]
                         + [ref[0, bb, pl.ds(base, 8), sl] for ref in (w_ref, k_ref, b_ref)])
        y_tiles = [jnp.zeros((8, 128), F32) for _ in chains]
        st = [s_ref[c] for c in range(n_ch)]
        dots = jnp.dot(jnp.concatenate([hilo(tl[5] * tl[2]) for tl in tiles]
                                       + [hilo(tl[4] * tl[2]) for tl in tiles], axis=0),
                       ones2, preferred_element_type=F32)
        wr = [tl[3] * tl[2] + tl[0] * dots[c * 8:(c + 1) * 8] for c, tl in enumerate(tiles)]
        kr = [dots[(n_ch + c) * 8:(n_ch + c + 1) * 8] for c in range(n_ch)]
        for tt in (reversed(range(8)) if reverse else range(8)):
            rows = [[t[tt:tt + 1, :] for t in tl] for tl in tiles]
            for sub_grp in range(0, n_ch, n_sub):
                cs = list(range(sub_grp, sub_grp + n_sub))
                lhs = pairs([(st[c] * rows[c][0]).astype(BF16) for c in cs])
                lhs += pairs([sel_b * rows[c][1].astype(BF16) for c in cs])
                lhs += pairs([(st[c] * wr[c][tt:tt + 1, :]).astype(BF16) for c in cs])
                red = jnp.dot(jnp.concatenate(lhs, axis=0), ones_bd, preferred_element_type=F32)
                for i, c in enumerate(cs):
                    _, v_t, _, w_t, k_t, b_t = rows[c]
                    j, side = i % half, slice((i // half) * 128, (i // half + 1) * 128)
                    sa = red[j * HEAD_DIM:(j + 1) * HEAD_DIM, side]
                    vb = red[(half + j) * HEAD_DIM:(half + j + 1) * HEAD_DIM, side]
                    rz = red[(2 * half + j) * HEAD_DIM:(2 * half + j + 1) * HEAD_DIM, side]
                    st[c] = st[c] * w_t + sa * b_t + vb * k_t
                    y_row = jnp.sum(rz * sel, axis=0, keepdims=True) + v_t * kr[c][tt:tt + 1, :]
                    y_tiles[c] = jnp.where(sub8 == tt, y_row, y_tiles[c])
        for c, (bb, p) in enumerate(chains):
            y_ref[bb, pl.ds(base, 8), p * 128:(p + 1) * 128] = y_tiles[c]
            s_ref[c] = st[c]
        return carry

    lax.fori_loop(0, n_grp, group, 0)


def _wkv(r, v, a, dec, km, bb, n_lat_tok, reverse):
    bsz, n_tok, _ = r.shape
    gb = 4 if bsz % 4 == 0 else (2 if bsz % 2 == 0 else 1)
    n_blk = n_tok // WKV_TBLK
    n_lat = n_lat_tok // WKV_TBLK
    n_ctx = n_blk - n_lat
    d = 1 if reverse else 0

    def tb(j):
        return n_blk - 1 - j if reverse else jnp.where(j < n_ctx, n_lat + j, j - n_ctx)

    s1 = pl.BlockSpec((gb, WKV_TBLK, RWKV_DIM), lambda g, j: (g, tb(j), 0))
    s2 = pl.BlockSpec((1, gb, WKV_TBLK, RWKV_DIM), lambda g, j: (d, g, tb(j), 0))
    return pl.pallas_call(
        functools.partial(_wkv_kernel, gb=gb, reverse=reverse),
        grid=(bsz // gb, n_blk),
        in_specs=[s1, s1, s1, s2, s2, s2],
        out_specs=s1,
        out_shape=jax.ShapeDtypeStruct((bsz, n_tok, RWKV_DIM), F32),
        scratch_shapes=[pltpu.VMEM((gb * (RWKV_DIM // 128), HEAD_DIM, 128), F32)],
        compiler_params=_cparams(("parallel", "arbitrary")),
        name="wkv_bwd" if reverse else "wkv_fwd",
    )(r, v, a, dec, km, bb)


def _attn_kernel(lam_ref, qr_ref, qp_ref, kl_ref, kc_ref, vl_ref, vc_ref, o_ref, *, latent):
    lam = lam_ref[0]
    lane = lax.broadcasted_iota(jnp.int32, (1, 128), 1)
    nt = (((1,), (1,)), ((), ()))
    segs = [(False, 0, kc_ref.shape[1])]
    if latent:
        seg_len = min(ATTN_KEY_SEG, kl_ref.shape[1])
        segs += [(True, off, seg_len) for off in range(0, kl_ref.shape[1], seg_len)]
    acc = [None] * len(segs)
    for m in range(2):
        own = (lane // HEAD_DIM) == m
        qp = jnp.where(own, qp_ref[0], jnp.zeros_like(qp_ref[0]))
        qr = jnp.where(own, qr_ref[0], jnp.zeros_like(qr_ref[0]))
        s = [lax.dot_general(qr if rot else qp, (kl_ref if rot else kc_ref)[0, off:off + n, :], nt,
                             preferred_element_type=F32) for rot, off, n in segs]
        mx = functools.reduce(jnp.maximum, [jnp.max(t, axis=-1, keepdims=True) for t in s])
        e = [jnp.exp(t - mx) for t in s]
        den = functools.reduce(jnp.add, [jnp.sum(t, axis=-1, keepdims=True) for t in e])
        coef = (1.0 / den) * (1.0 if m == 0 else -lam)
        acc = [t * coef if a is None else a + t * coef for a, t in zip(acc, e)]
    o_ref[0] = functools.reduce(jnp.add, [
        jnp.dot(a.astype(BF16), (vl_ref if rot else vc_ref)[0, off:off + n, :], preferred_element_type=F32)
        for a, (rot, off, n) in zip(acc, segs)])


def _attn(lam, qr, qp, ks, vd, n_lat_tok, latent):
    bsz, n_tok, _ = qr.shape
    n_ctx_tok = n_tok - n_lat_tok
    ctx_blk = n_lat_tok // n_ctx_tok
    if latent:
        qb, n_q, q_off = TM, n_lat_tok // TM, 0
    else:
        qb, n_q, q_off = n_ctx_tok, 1, ctx_blk
    qspec = pl.BlockSpec((1, qb, 128), lambda b, h, i: (b, i + q_off, h))
    lat = pl.BlockSpec((1, n_lat_tok, 128), lambda b, h, i: (b, 0, h))
    ctx = pl.BlockSpec((1, n_ctx_tok, 128), lambda b, h, i: (b, ctx_blk, h))
    return pl.pallas_call(
        functools.partial(_attn_kernel, latent=latent),
        grid=(bsz, DIFF_HEADS, n_q),
        in_specs=[pl.BlockSpec(memory_space=pltpu.SMEM), qspec, qspec, lat, ctx, lat, ctx],
        out_specs=pl.BlockSpec((1, qb, 128), lambda b, h, i: (b, i, h)),
        out_shape=jax.ShapeDtypeStruct((bsz, n_q * qb, DIFF_W), F32),
        compiler_params=_cparams(("parallel", "parallel", "arbitrary")),
        name="attn_lat" if latent else "attn_ctx",
    )(lam, qr, qp, ks, ks, vd, vd)


def _hyb_out_kernel(x_ref, mod_ref, yf_ref, yb_ref, r_ref, v_ref, km_ref, g_ref, ol_ref, oc_ref, ones_ref,
                    rk_ref, lnw_ref, lnb_ref, sg_ref, wo_ref, out_ref, *, n_lat_tiles, sub_scale):
    i = pl.program_id(1)
    ones = ones_ref[...]
    y = yf_ref[0] + yb_ref[0]
    mean = _segsum(y, ones) * (1.0 / HEAD_DIM)
    yc = y - mean
    var = _segsum(yc * yc, ones) * (1.0 / HEAD_DIM)
    yn = yc * lax.rsqrt(var + RWKV_GN_EPS) * lnw_ref[...] + lnb_ref[...]
    r = r_ref[0]
    kb = 0.5 * (km_ref[0, 0] + km_ref[1, 0])
    bonus = _segsum(r * kb * rk_ref[...], ones) * v_ref[0]
    a_mix = ((yn + bonus) * g_ref[0]).astype(BF16)

    o = jnp.where(i < n_lat_tiles, ol_ref[0], oc_ref[0])
    subs = []
    for h in range(DIFF_HEADS):
        oh = o[:, h * DIFF_V:(h + 1) * DIFF_V]
        ms = jnp.mean(oh * oh, axis=-1, keepdims=True)
        subs.append(oh * lax.rsqrt(ms + NORM_EPS) * sg_ref[...] * sub_scale)
    sub = jnp.concatenate(subs, axis=1).astype(BF16)
    mix = (jnp.dot(a_mix, wo_ref[0:RWKV_DIM, :], preferred_element_type=F32)
           + jnp.dot(sub, wo_ref[RWKV_DIM:, :], preferred_element_type=F32))
    out_ref[0] = x_ref[0] + mod_ref[0, 2:3, :] * mix


def _hyb_out(x, mod, yf, yb, r, v, km, g, o_lat, o_ctx, ones64, r_k, ln_w, ln_b, subln_g, w_out,
             n_lat_tiles, sub_scale):
    bsz, n_tok, _ = x.shape
    n_tiles = n_tok // TM
    t1 = lambda w: pl.BlockSpec((1, TM, w), lambda b, i: (b, i, 0))
    t2 = pl.BlockSpec((2, 1, TM, RWKV_DIM), lambda b, i: (0, b, i, 0))
    ol = pl.BlockSpec((1, TM, DIFF_W), lambda b, i: (b, jnp.minimum(i, n_lat_tiles - 1), 0))
    oc = pl.BlockSpec((1, TM, DIFF_W), lambda b, i: (b, 0, 0))
    return pl.pallas_call(
        functools.partial(_hyb_out_kernel, n_lat_tiles=n_lat_tiles, sub_scale=sub_scale),
        grid=(bsz, n_tiles),
        in_specs=[t1(D_MODEL), _mod_spec(n_lat_tiles), t1(RWKV_DIM), t1(RWKV_DIM), t1(RWKV_DIM), t1(RWKV_DIM), t2,
                  t1(RWKV_DIM), ol, oc, _const_spec(ones64.shape), _const_spec(r_k.shape),
                  _const_spec(ln_w.shape), _const_spec(ln_b.shape), _const_spec(subln_g.shape),
                  _const_spec(w_out.shape)],
        out_specs=t1(D_MODEL),
        out_shape=jax.ShapeDtypeStruct(x.shape, F32),
        compiler_params=_cparams(("parallel", "parallel")),
        name="hyb_out",
    )(x, mod, yf, yb, r, v, km, g, o_lat, o_ctx, ones64, r_k, ln_w, ln_b, subln_g, w_out)


def _mlp_kernel(x_ref, mod_ref, g_ref, w1_ref, w2_ref, gf_ref, out_ref, *, final_norm):
    x = x_ref[0]
    h = _rms_mod(x, g_ref[...], mod_ref[0, 3:4, :], mod_ref[0, 4:5, :]).astype(BF16)
    acc = jnp.zeros((TM, D_MODEL), F32)
    for c in range(FFN_DIM // FFN_CHUNK):
        cs = slice(c * FFN_CHUNK, (c + 1) * FFN_CHUNK)
        u = jnp.maximum(jnp.dot(h, w1_ref[:, cs], preferred_element_type=F32), 0.0)
        acc = acc + jnp.dot((u * u).astype(BF16), w2_ref[cs, :], preferred_element_type=F32)
    out = x + mod_ref[0, 5:6, :] * acc
    if final_norm:
        ms = jnp.mean(out * out, axis=-1, keepdims=True)
        out = out * lax.rsqrt(ms + NORM_EPS) * gf_ref[...]
    out_ref[0] = out


def _mlp(x, mod, norm_g, w1, w2, norm_f, n_lat_tiles, final_norm):
    bsz, n_tok, _ = x.shape
    t1 = pl.BlockSpec((1, TM, D_MODEL), lambda b, i: (b, i, 0))
    return pl.pallas_call(
        functools.partial(_mlp_kernel, final_norm=final_norm),
        grid=(bsz, n_tok // TM),
        in_specs=[t1, _mod_spec(n_lat_tiles), _const_spec(norm_g.shape), _const_spec(w1.shape),
                  _const_spec(w2.shape), _const_spec(norm_f.shape)],
        out_specs=t1,
        out_shape=jax.ShapeDtypeStruct(x.shape, F32),
        compiler_params=_cparams(("parallel", "parallel")),
        name="mlp",
    )(x, mod, norm_g, w1, w2, norm_f)


def _ssd_in_kernel(xp_ref, xc_ref, xn_ref, g_ref, mod_ref, wzx_ref, wdt_ref, cw_ref, cb_ref,
                   z_out, xs_out, xt_out, bm_out, cm_out, dt_out, hbuf, pbuf, *, n_lat_tiles):
    _normed_with_halo(xp_ref, xc_ref, xn_ref, g_ref, mod_ref, hbuf, n_lat_tiles)
    hc = hbuf[HALO:HALO + TM, :].astype(BF16)
    z_out[0] = jnp.dot(hc, wzx_ref[:, 0:D_INNER], preferred_element_type=F32)
    dt_out[0] = jnp.dot(hc, wdt_ref[...], preferred_element_type=F32)
    pbuf[...] = jnp.dot(hbuf[...].astype(BF16), wzx_ref[:, D_INNER:], preferred_element_type=F32)
    conv = cb_ref[...] + cw_ref[CONV_W // 2:CONV_W // 2 + 1, :] * pbuf[HALO:HALO + TM, :]
    for j in range(CONV_W):
        if j != CONV_W // 2:
            conv = conv + cw_ref[j:j + 1, :] * pbuf[pl.ds(HALO + j - CONV_W // 2, TM), :]
    act = _silu(conv)
    xs_out[0] = act[:, 0:D_INNER]
    for c in range(D_INNER // 128):
        xt_out[0, c * 128:(c + 1) * 128, :] = act[:, c * 128:(c + 1) * 128].T
    bm_out[0] = act[:, D_INNER:D_INNER + SSD_GROUPS * D_STATE].astype(BF16)
    cm_out[0] = act[:, D_INNER + SSD_GROUPS * D_STATE:].astype(BF16)


def _ssd_in(x, norm_g, mod, w_zx, w_dt, conv_w, conv_b, n_lat_tiles):
    bsz, n_tok, _ = x.shape
    prev, cur, nxt = _halo_specs(n_lat_tiles, D_MODEL)
    t1 = lambda w: pl.BlockSpec((1, TM, w), lambda b, i: (b, i, 0))
    sh = lambda w, dt: jax.ShapeDtypeStruct((bsz, n_tok, w), dt)
    gw = SSD_GROUPS * D_STATE
    return pl.pallas_call(
        functools.partial(_ssd_in_kernel, n_lat_tiles=n_lat_tiles),
        grid=(bsz, n_tok // TM),
        in_specs=[prev, cur, nxt, _const_spec((1, D_MODEL)), _mod_spec(n_lat_tiles),
                  _const_spec(w_zx.shape), _const_spec(w_dt.shape), _const_spec(conv_w.shape),
                  _const_spec(conv_b.shape)],
        out_specs=[t1(D_INNER), t1(D_INNER), pl.BlockSpec((1, D_INNER, TM), lambda b, i: (b, 0, i)),
                   t1(gw), t1(gw), t1(128)],
        out_shape=[sh(D_INNER, F32), sh(D_INNER, F32), jax.ShapeDtypeStruct((bsz, D_INNER, n_tok), F32),
                   sh(gw, BF16), sh(gw, BF16), sh(128, F32)],
        scratch_shapes=[pltpu.VMEM((TM + 2 * HALO, D_MODEL), F32),
                        pltpu.VMEM((TM + 2 * HALO, CONV_DIM), F32)],
        compiler_params=_cparams(("parallel", "parallel")),
        name="ssd_in",
    )(x, x, x, norm_g, mod, w_zx, w_dt, conv_w, conv_b)


def _ssd_scan_kernel(xt_ref, bm_ref, cm_ref, dt_ref, dtb_ref, alog_ref, y_ref, h_ref):
    d = pl.program_id(0)
    j = pl.program_id(2)

    @pl.when(j == 0)
    def _():
        h_ref[...] = jnp.zeros_like(h_ref)

    row = lax.broadcasted_iota(jnp.int32, (CHUNK, CHUNK), 0)
    col = lax.broadcasted_iota(jnp.int32, (CHUNK, CHUNK), 1)
    keep = jnp.where(d == 0, row - col, col - row) >= 0
    dtd = _softplus(dt_ref[0] + dtb_ref[0])
    dta = dtd * (-jnp.exp(alog_ref[0]))
    cs = jnp.dot(keep.astype(F32), dta, precision=lax.Precision.HIGHEST, preferred_element_type=F32)
    cs_t = cs.T
    dtd_t = dtd.T
    tot_c = jnp.where(d == 0, cs_t[:, CHUNK - 1:CHUNK], cs_t[:, 0:1])
    coef_t = dtd_t * jnp.exp(tot_c - cs_t)
    e_tot = jnp.exp(jnp.broadcast_to(tot_c, (128, D_STATE)))
    nt = (((1,), (1,)), ((), ()))
    zero = jnp.zeros((HEAD_DIM, 2 * CHUNK), BF16)

    def rows2(t, q):
        return jnp.concatenate([jnp.broadcast_to(t[2 * q + i:2 * q + i + 1, :], (HEAD_DIM, t.shape[1]))
                                for i in range(2)], axis=0)

    for g in range(SSD_GROUPS):
        bc = bm_ref[0, :, g * D_STATE:(g + 1) * D_STATE]
        cc = cm_ref[0, :, g * D_STATE:(g + 1) * D_STATE]
        cb = lax.dot_general(cc, bc, nt, preferred_element_type=F32)
        cc_f = cc.astype(F32)
        for qq in range(SSD_HPG // 2):
            q = g * (SSD_HPG // 2) + qq
            xt = xt_ref[0, q * 128:(q + 1) * 128, :]
            xdt_t = (xt * rows2(dtd_t, q)).astype(BF16)
            xw_t = (xt * rows2(coef_t, q)).astype(BF16)
            state = h_ref[q]
            st_b = state.astype(BF16)
            lhs = []
            for i in range(2):
                h = 2 * q + i
                cs_col = jnp.broadcast_to(cs[:, h:h + 1], (CHUNK, CHUNK))
                lmat = jnp.exp(jnp.where(keep, cs_col - cs_t[h:h + 1, :], -jnp.inf))
                lhs += [(cb * lmat).astype(BF16), (cc_f * jnp.exp(cs_col)).astype(BF16)]
            w_top = jnp.concatenate([xdt_t[0:HEAD_DIM], st_b[0:HEAD_DIM], zero], axis=1)
            w_bot = jnp.concatenate([zero, xdt_t[HEAD_DIM:], st_b[HEAD_DIM:]], axis=1)
            y_ref[0, 0, :, q * 128:(q + 1) * 128] = lax.dot_general(
                jnp.concatenate(lhs, axis=1), jnp.concatenate([w_top, w_bot], axis=0), nt,
                preferred_element_type=F32)
            h_ref[q] = state * rows2(e_tot, q) + jnp.dot(xw_t, bc, preferred_element_type=F32)


def _ssd_scan(xs_t, bm, cm, dt, dt_bias, a_log, n_lat_tok):
    bsz, _, n_tok = xs_t.shape
    n_blk = n_tok // CHUNK
    n_lat = n_lat_tok // CHUNK
    n_ctx = n_blk - n_lat

    def tb(d, j):
        fwd = jnp.where(j < n_ctx, n_lat + j, j - n_ctx)
        return jnp.where(d == 0, fwd, n_blk - 1 - j)

    t1 = lambda w: pl.BlockSpec((1, CHUNK, w), lambda d, b, j: (b, tb(d, j), 0))
    par = pl.BlockSpec((1, 1, 128), lambda d, b, j: (d, 0, 0))
    return pl.pallas_call(
        _ssd_scan_kernel,
        grid=(2, bsz, n_blk),
        in_specs=[pl.BlockSpec((1, D_INNER, CHUNK), lambda d, b, j: (b, 0, tb(d, j))),
                  t1(SSD_GROUPS * D_STATE), t1(SSD_GROUPS * D_STATE), t1(128), par, par],
        out_specs=pl.BlockSpec((1, 1, CHUNK, D_INNER), lambda d, b, j: (d, b, tb(d, j), 0)),
        out_shape=jax.ShapeDtypeStruct((2, bsz, n_tok, D_INNER), F32),
        scratch_shapes=[pltpu.VMEM((SSD_HEADS // 2, 2 * HEAD_DIM, D_STATE), F32)],
        compiler_params=_cparams(("parallel", "parallel", "arbitrary")),
        name="ssd_scan",
    )(xs_t, bm, cm, dt, dt_bias, a_log)


def _ssd_out_kernel(x_ref, mod_ref, y_ref, xs_ref, z_ref, dsk_ref, ng_ref, wo_ref, out_ref):
    y = y_ref[0, 0] + y_ref[1, 0] + dsk_ref[...] * xs_ref[0]
    yg = y * _silu(z_ref[0])
    gw = D_INNER // SSD_GROUPS
    parts = []
    for g in range(SSD_GROUPS):
        t = yg[:, g * gw:(g + 1) * gw]
        ms = jnp.mean(t * t, axis=-1, keepdims=True)
        parts.append((t * lax.rsqrt(ms + NORM_EPS) * ng_ref[:, g * gw:(g + 1) * gw]).astype(BF16))
    yn = jnp.concatenate(parts, axis=1)
    out_ref[0] = x_ref[0] + mod_ref[0, 2:3, :] * jnp.dot(yn, wo_ref[...], preferred_element_type=F32)


def _ssd_out(x, mod, y, xs, z, d_skip, norm_g, w_out, n_lat_tiles):
    bsz = x.shape[0]
    t1 = lambda w: pl.BlockSpec((1, TM, w), lambda b, i: (b, i, 0))
    return pl.pallas_call(
        _ssd_out_kernel,
        grid=(bsz, n_lat_tiles),
        in_specs=[t1(D_MODEL), _mod_spec(n_lat_tiles),
                  pl.BlockSpec((2, 1, TM, D_INNER), lambda b, i: (0, b, i, 0)),
                  t1(D_INNER), t1(D_INNER), _const_spec(d_skip.shape), _const_spec(norm_g.shape),
                  _const_spec(w_out.shape)],
        out_specs=t1(D_MODEL),
        out_shape=jax.ShapeDtypeStruct((bsz, n_lat_tiles * TM, D_MODEL), F32),
        compiler_params=_cparams(("parallel", "parallel")),
        name="ssd_out",
    )(x, mod, y, xs, z, d_skip, norm_g, w_out)


def _rope_tables(t_len, n_ctx_tok):
    n_rows = t_len // GRID_W
    rows = jnp.broadcast_to(jnp.arange(n_rows)[:, None], (n_rows, GRID_W)).reshape(-1)
    cols = jnp.broadcast_to(jnp.arange(GRID_W)[None, :], (n_rows, GRID_W)).reshape(-1)
    inv = ROPE_BASE ** (-jnp.arange(0, HEAD_DIM // 2, 2, dtype=F32) / (HEAD_DIM // 2))
    ang_r = rows.astype(F32)[:, None] * inv
    ang_c = cols.astype(F32)[:, None] * inv
    cos = jnp.concatenate([jnp.cos(ang_r)] * 2 + [jnp.cos(ang_c)] * 2, axis=1)
    sin = jnp.concatenate([-jnp.sin(ang_r), jnp.sin(ang_r), -jnp.sin(ang_c), jnp.sin(ang_c)], axis=1)
    reps = DIFF_W // HEAD_DIM
    cos = jnp.concatenate([jnp.tile(cos, (1, reps)), jnp.ones((n_ctx_tok, DIFF_W), F32)], axis=0)
    sin = jnp.concatenate([jnp.tile(sin, (1, reps)), jnp.zeros((n_ctx_tok, DIFF_W), F32)], axis=0)
    return cos, sin


def kernel(x, c, ctx, c_ctx, ada_w, ada_b, norm1_g, norm2_g, mlp_w1, mlp_w2, hy_w_in, hy_w_out, rwkv_mu, rwkv_w0, rwkv_w_up, rwkv_a0, rwkv_a_up, rwkv_g_up, rwkv_k_k, rwkv_k_a, rwkv_r_k, rwkv_ln_w, rwkv_ln_b, diff_lq1, diff_lk1, diff_lq2, diff_lk2, diff_subln_g, ssd_w_in, ssd_conv_w, ssd_conv_b, ssd_dt_bias, ssd_a_log, ssd_d, ssd_norm_g, ssd_w_out, norm_f_g):
    bsz, t_len, _ = x.shape
    l_ctx = ctx.shape[1]
    assert l_ctx == TM and t_len % TM == 0 and t_len % l_ctx == 0
    depth = ada_w.shape[0]
    n_lat_tiles = t_len // TM
    row = lambda t: t.reshape(1, -1)

    cc = jnp.zeros((16, D_MODEL), F32).at[0:bsz].set(c).at[bsz].set(c_ctx)
    mods = _ada(cc, ada_w, ada_b)
    mod_lat = mods[:, 0:bsz]
    mod_ctx = jnp.broadcast_to(mods[:, bsz:bsz + 1], mod_lat.shape)
    mods = jnp.stack([mod_lat, mod_ctx], axis=2).reshape(depth, bsz * 2, 6, D_MODEL)

    xa = jnp.concatenate([x, ctx], axis=1)
    ones64 = (jnp.arange(RWKV_DIM)[:, None] // HEAD_DIM == jnp.arange(RWKV_DIM)[None, :] // HEAD_DIM).astype(BF16)
    cos_t, sin_t = _rope_tables(t_len, l_ctx)

    for li in range(depth):
        last = li == depth - 1
        mod = mods[li]
        if li % 2 == 0:
            e = li // 2
            lam_init = 0.8 - 0.6 * math.exp(-0.3 * li)
            lam = (jnp.exp(jnp.sum(diff_lq1[e] * diff_lk1[e])) - jnp.exp(jnp.sum(diff_lq2[e] * diff_lk2[e]))
                   + lam_init).reshape(1).astype(F32)
            zero = jnp.zeros((64, RWKV_DIM), F32)
            lora = jnp.stack([jnp.concatenate(
                [jnp.concatenate([rwkv_w_up[e, d], zero], axis=1),
                 jnp.concatenate([zero, rwkv_a_up[e, d]], axis=1)], axis=0) for d in range(2)]).astype(BF16)
            (r, v, a, dec, km, bb, g, qr, qp, ks, vd) = _hyb_in(
                xa, row(norm1_g[li]), mod, hy_w_in[e].astype(BF16), row(rwkv_mu[e]), row(rwkv_k_k[e]),
                row(rwkv_k_a[e]), rwkv_w0[e], rwkv_a0[e], lora, rwkv_g_up[e].astype(BF16), ones64,
                cos_t, sin_t, n_lat_tiles)
            yf = _wkv(r, v, a, dec, km, bb, t_len, False)
            yb = _wkv(r, v, a, dec, km, bb, t_len, True)
            o_lat = _attn(lam, qr, qp, ks, vd, t_len, True)
            o_ctx = _attn(lam, qr, qp, ks, vd, t_len, False)
            xa = _hyb_out(xa, mod, yf, yb, r, v, km, g, o_lat, o_ctx, ones64, row(rwkv_r_k[e]),
                          row(rwkv_ln_w[e]), row(rwkv_ln_b[e]),
                          row(diff_subln_g[e]), hy_w_out[e].astype(BF16), n_lat_tiles, 1.0 - lam_init)
        else:
            o = li // 2
            w_in = ssd_w_in[o]
            w_zx = w_in[:, 0:D_INNER + CONV_DIM].astype(BF16)
            w_dt = jnp.pad(w_in[:, D_INNER + CONV_DIM:], ((0, 0), (0, 128 - SSD_HEADS))).astype(BF16)
            pad_h = lambda t: jnp.pad(t, ((0, 0), (0, 128 - SSD_HEADS))).reshape(2, 1, 128)
            z, xs, xs_t, bm, cm, dt = _ssd_in(xa, row(norm1_g[li]), mod, w_zx, w_dt, ssd_conv_w[o],
                                              row(ssd_conv_b[o]), n_lat_tiles)
            y = _ssd_scan(xs_t, bm, cm, dt, pad_h(ssd_dt_bias[o]), pad_h(ssd_a_log[o]), t_len)
            d_skip = row(jnp.repeat(ssd_d[o], HEAD_DIM))
            if last:
                xa = _ssd_out(xa, mod, y, xs, z, d_skip, row(ssd_norm_g[o]), ssd_w_out[o].astype(BF16),
                              n_lat_tiles)
            else:
                raise NotImplementedError("an SSD layer that is not the last layer")
        xa = _mlp(xa, mod, row(norm2_g[li]), mlp_w1[li].astype(BF16), mlp_w2[li].astype(BF16),
                  row(norm_f_g), n_lat_tiles, last)
    return xa[:, 0:t_len]
```

```python
import functools
import math

import jax
import jax.numpy as jnp
from jax import lax
from jax.experimental import pallas as pl
from jax.experimental.pallas import tpu as pltpu

F32 = jnp.float32
BF16 = jnp.bfloat16

D_MODEL = 1024
FFN_DIM = 4 * D_MODEL
NORM_EPS = 1e-6
GRID_W = 64
ROPE_BASE = 10000.0

HEAD_DIM = 64
RWKV_DIM = 512
RWKV_HEADS = 8
RWKV_COLS = 1792
RWKV_GN_EPS = 64e-5
DIFF_HEADS = 4
DIFF_V = 128
DIFF_W = 512
HYB_IN = 3328

D_INNER = 2048
SSD_HEADS = 32
SSD_GROUPS = 8
SSD_HPG = 4
D_STATE = 128
CONV_W = 5
CONV_DIM = 4096
CHUNK = 128

TM = 256
HALO = 8
FFN_CHUNK = 1024
WKV_TBLK = 128
WKV_SUB = 8
VMEM_LIMIT = 56 * 1024 * 1024


def _cparams(sem):
    return pltpu.CompilerParams(dimension_semantics=sem, vmem_limit_bytes=VMEM_LIMIT)


def _const_spec(shape):
    nd = len(shape)
    return pl.BlockSpec(shape, lambda *_: (0,) * nd, pipeline_mode=pl.Buffered(1))


def _split2(x):
    hi = x.astype(BF16)
    lo = (x - hi.astype(F32)).astype(BF16)
    return hi, lo


def _segsum(x, ones):
    hi, lo = _split2(x)
    return (jnp.dot(hi, ones, preferred_element_type=F32)
            + jnp.dot(lo, ones, preferred_element_type=F32))


def _rms_mod(x, g, shift, scale):
    ms = jnp.mean(x * x, axis=-1, keepdims=True)
    return (x * lax.rsqrt(ms + NORM_EPS) * g) * (1.0 + scale) + shift


def _sigmoid(x):
    return 1.0 / (1.0 + jnp.exp(-x))


def _softplus(x):
    return jnp.maximum(x, 0.0) + jnp.log(1.0 + jnp.exp(-jnp.abs(x)))


def _silu(x):
    return x * _sigmoid(x)


def _ada_kernel(c_ref, w_ref, b_ref, o_ref):
    o_ref[0] = jnp.dot(_silu(c_ref[...]), w_ref[0], precision=lax.Precision.HIGHEST,
                       preferred_element_type=F32) + b_ref[0]


def _ada(cc, ada_w, ada_b):
    depth, d, n = ada_w.shape
    tn = 1536
    return pl.pallas_call(
        _ada_kernel,
        grid=(depth, n // tn),
        in_specs=[_const_spec(cc.shape),
                  pl.BlockSpec((1, d, tn), lambda l, j: (l, 0, j)),
                  pl.BlockSpec((1, 1, tn), lambda l, j: (l, 0, j))],
        out_specs=pl.BlockSpec((1, cc.shape[0], tn), lambda l, j: (l, 0, j)),
        out_shape=jax.ShapeDtypeStruct((depth, cc.shape[0], n), F32),
        compiler_params=_cparams(("parallel", "parallel")),
        name="ada",
    )(cc, ada_w, ada_b.reshape(depth, 1, n))


def _halo_specs(n_lat_tiles, width):
    per = TM // HALO
    last = (n_lat_tiles + 1) * per - 1
    prev = pl.BlockSpec((1, HALO, width), lambda b, i: (b, jnp.maximum(i * per - 1, 0), 0))
    cur = pl.BlockSpec((1, TM, width), lambda b, i: (b, i, 0))
    nxt = pl.BlockSpec((1, HALO, width), lambda b, i: (b, jnp.minimum((i + 1) * per, last), 0))
    return prev, cur, nxt


def _mod_spec(n_lat_tiles):
    return pl.BlockSpec((1, 6, D_MODEL), lambda b, i: (b * 2 + (i >= n_lat_tiles).astype(jnp.int32), 0, 0))


def _normed_with_halo(xp_ref, xc_ref, xn_ref, g_ref, mod_ref, hbuf, n_lat_tiles):
    i = pl.program_id(1)
    g = g_ref[...]
    shift, scale = mod_ref[0, 0:1, :], mod_ref[0, 1:2, :]
    prev_ok = jnp.logical_and(i >= 1, i < n_lat_tiles).astype(F32)
    next_ok = (i < n_lat_tiles - 1).astype(F32)
    hbuf[0:HALO, :] = _rms_mod(xp_ref[0], g, shift, scale) * prev_ok
    hbuf[HALO:HALO + TM, :] = _rms_mod(xc_ref[0], g, shift, scale)
    hbuf[HALO + TM:, :] = _rms_mod(xn_ref[0], g, shift, scale) * next_ok


def _hyb_in_kernel(xp_ref, xc_ref, xn_ref, g_ref, mod_ref, w_ref, mu_ref, kk_w_ref, ka_ref,
                   w0_ref, a0_ref, lora_ref, gup_ref, ones_ref, cos_ref, sin_ref,
                   r_out, v_out, a_out, dec_out, km_out, bb_out, g_out,
                   qr_out, qp_out, ks_out, vd_out, hbuf, pbuf, *, n_lat_tiles):
    _normed_with_halo(xp_ref, xc_ref, xn_ref, g_ref, mod_ref, hbuf, n_lat_tiles)
    pbuf[...] = jnp.dot(hbuf[...].astype(BF16), w_ref[...], preferred_element_type=F32)

    p = pbuf[HALO:HALO + TM, 0:RWKV_COLS]
    prev = pbuf[pl.ds(HALO - 1, TM), 0:RWKV_COLS]
    nxt = pbuf[pl.ds(HALO + 1, TM), 0:RWKV_COLS]
    p = p + (0.5 * (prev + nxt) - p) * mu_ref[...]
    r = p[:, 0:RWKV_DIM]
    k = p[:, RWKV_DIM:2 * RWKV_DIM]
    v = p[:, 2 * RWKV_DIM:3 * RWKV_DIM]
    lo_in = p[:, 3 * RWKV_DIM:3 * RWKV_DIM + 128]
    gd = p[:, 3 * RWKV_DIM + 128:RWKV_COLS]
    kkf = k * kk_w_ref[...]
    kk = kkf * lax.rsqrt(_segsum(kkf * kkf, ones_ref[...]) + 1e-12)
    r_out[0] = r
    v_out[0] = v
    a_out[0] = -kk
    g_out[0] = jnp.dot(_sigmoid(gd).astype(BF16), gup_ref[...], preferred_element_type=F32)
    lane = lax.broadcasted_iota(jnp.int32, lo_in.shape, 1)
    lo_act = jnp.where(lane < 64, jnp.tanh(lo_in), lo_in).astype(BF16)
    for d in range(2):
        up = jnp.dot(lo_act, lora_ref[d], preferred_element_type=F32)
        wlog = -_softplus(-(w0_ref[d:d + 1, :] + up[:, 0:RWKV_DIM])) - 0.5
        dec_out[d, 0] = jnp.exp(-jnp.exp(wlog))
        a = _sigmoid(a0_ref[d:d + 1, :] + up[:, RWKV_DIM:])
        km_out[d, 0] = k * (1.0 + (a - 1.0) * ka_ref[...])
        bb_out[d, 0] = kk * a

    o = RWKV_COLS
    vd_out[0] = pbuf[HALO:HALO + TM, o + 2 * DIFF_W:o + 3 * DIFF_W].astype(BF16)
    lane = lax.broadcasted_iota(jnp.int32, (TM, 128), 1)
    first = (lane % 32) < 16
    for s in range(DIFF_W // 128):
        ls = slice(s * 128, (s + 1) * 128)
        cos, sin = cos_ref[:, ls], sin_ref[:, ls]

        def rope(t):
            partner = jnp.where(first, pltpu.roll(t, 128 - 16, 1), pltpu.roll(t, 16, 1))
            return t * cos + partner * sin

        q = pbuf[HALO:HALO + TM, o + s * 128:o + (s + 1) * 128] * (HEAD_DIM ** -0.5)
        kd = pbuf[HALO:HALO + TM, o + DIFF_W + s * 128:o + DIFF_W + (s + 1) * 128]
        qp_out[0, :, ls] = q.astype(BF16)
        qr_out[0, :, ls] = rope(q).astype(BF16)
        ks_out[0, :, ls] = rope(kd).astype(BF16)


def _hyb_in(x, norm_g, mod, w_in, mu, k_k, k_a, w0, a0, lora, g_up, ones64, cos_t, sin_t, n_lat_tiles):
    bsz, n_tok, _ = x.shape
    n_tiles = n_tok // TM
    prev, cur, nxt = _halo_specs(n_lat_tiles, D_MODEL)
    tile = lambda w, dt: jax.ShapeDtypeStruct((bsz, n_tok, w), dt)
    tile2 = lambda w, dt: jax.ShapeDtypeStruct((2, bsz, n_tok, w), dt)
    o1 = pl.BlockSpec((1, TM, RWKV_DIM), lambda b, i: (b, i, 0))
    o2 = pl.BlockSpec((2, 1, TM, RWKV_DIM), lambda b, i: (0, b, i, 0))
    tab = pl.BlockSpec((TM, DIFF_W), lambda b, i: (i, 0))
    return pl.pallas_call(
        functools.partial(_hyb_in_kernel, n_lat_tiles=n_lat_tiles),
        grid=(bsz, n_tiles),
        in_specs=[prev, cur, nxt, _const_spec((1, D_MODEL)), _mod_spec(n_lat_tiles),
                  _const_spec(w_in.shape), _const_spec(mu.shape), _const_spec(k_k.shape),
                  _const_spec(k_a.shape), _const_spec(w0.shape), _const_spec(a0.shape),
                  _const_spec(lora.shape), _const_spec(g_up.shape), _const_spec(ones64.shape), tab, tab],
        out_specs=[o1, o1, o1, o2, o2, o2, o1, o1, o1, o1, o1],
        out_shape=[tile(RWKV_DIM, F32)] * 3 + [tile2(RWKV_DIM, F32)] * 3 + [tile(RWKV_DIM, F32)]
                  + [tile(DIFF_W, BF16)] * 4,
        scratch_shapes=[pltpu.VMEM((TM + 2 * HALO, D_MODEL), F32),
                        pltpu.VMEM((TM + 2 * HALO, HYB_IN), F32)],
        compiler_params=_cparams(("parallel", "parallel")),
        name="hyb_in",
    )(x, x, x, norm_g, mod, w_in, mu, k_k, k_a, w0, a0, lora, g_up, ones64, cos_t, sin_t)


def _wkv_kernel(r_ref, v_ref, a_ref, w_ref, k_ref, b_ref, y_ref, s_ref, *, gb, reverse):
    @pl.when(pl.program_id(1) == 0)
    def _():
        s_ref[...] = jnp.zeros_like(s_ref)

    lane = lax.broadcasted_iota(jnp.int32, (HEAD_DIM, 128), 1)
    sub = lax.broadcasted_iota(jnp.int32, (HEAD_DIM, 128), 0)
    sel = (lane % HEAD_DIM == sub).astype(F32)
    row = lax.broadcasted_iota(jnp.int32, (256, 128), 0)
    col = lax.broadcasted_iota(jnp.int32, (256, 128), 1)
    ones2 = ((row % 128) // HEAD_DIM == col // HEAD_DIM).astype(BF16)
    sub8 = lax.broadcasted_iota(jnp.int32, (8, 128), 0)
    sel_b = sel.astype(BF16)
    row = lax.broadcasted_iota(jnp.int32, (256, 256), 0)
    col = lax.broadcasted_iota(jnp.int32, (256, 256), 1)
    ones_bd = (row // HEAD_DIM == col // HEAD_DIM).astype(BF16)
    n_grp = WKV_TBLK // 8
    n_pair = RWKV_DIM // 128
    chains = [(bb, p) for bb in range(gb) for p in range(n_pair)]
    n_ch = len(chains)
    n_sub = min(n_ch, WKV_SUB)
    half = n_sub // 2

    def hilo(x):
        hi, lo = _split2(x)
        return jnp.concatenate([hi, lo], axis=1)

    def pairs(xs):
        return [jnp.concatenate([xs[i], xs[i + half]], axis=1) for i in range(half)]

    def group(i, carry):
        base = pl.multiple_of((n_grp - 1 - i if reverse else i) * 8, 8)
        tiles = []
        for bb, p in chains:
            sl = slice(p * 128, (p + 1) * 128)
            tiles.append([ref[bb, pl.ds(base, 8), sl] for ref in (a_ref, v_ref, r_ref)]
                         + [ref[0, bb, pl.ds(base, 8), sl] for ref in (w_ref, k_ref, b_ref)])
        y_tiles = [jnp.zeros((8, 128), F32) for _ in chains]
        st = [s_ref[c] for c in range(n_ch)]
        dots = jnp.dot(jnp.concatenate([hilo(tl[5] * tl[2]) for tl in tiles]
                                       + [hilo(tl[4] * tl[2]) for tl in tiles], axis=0),
                       ones2, preferred_element_type=F32)
        wr = [tl[3] * tl[2] + tl[0] * dots[c * 8:(c + 1) * 8] for c, tl in enumerate(tiles)]
        kr = [dots[(n_ch + c) * 8:(n_ch + c + 1) * 8] for c in range(n_ch)]
        for tt in (reversed(range(8)) if reverse else range(8)):
            rows = [[t[tt:tt + 1, :] for t in tl] for tl in tiles]
            for sub_grp in range(0, n_ch, n_sub):
                cs = list(range(sub_grp, sub_grp + n_sub))
                lhs = pairs([(st[c] * rows[c][0]).astype(BF16) for c in cs])
                lhs += pairs([sel_b * rows[c][1].astype(BF16) for c in cs])
                lhs += pairs([(st[c] * wr[c][tt:tt + 1, :]).astype(BF16) for c in cs])
                red = jnp.dot(jnp.concatenate(lhs, axis=0), ones_bd, preferred_element_type=F32)
                for i, c in enumerate(cs):
                    _, v_t, _, w_t, k_t, b_t = rows[c]
                    j, side = i % half, slice((i // half) * 128, (i // half + 1) * 128)
                    sa = red[j * HEAD_DIM:(j + 1) * HEAD_DIM, side]
                    vb = red[(half + j) * HEAD_DIM:(half + j + 1) * HEAD_DIM, side]
                    rz = red[(2 * half + j) * HEAD_DIM:(2 * half + j + 1) * HEAD_DIM, side]
                    st[c] = st[c] * w_t + sa * b_t + vb * k_t
                    y_row = jnp.sum(rz * sel, axis=0, keepdims=True) + v_t * kr[c][tt:tt + 1, :]
                    y_tiles[c] = jnp.where(sub8 == tt, y_row, y_tiles[c])
        for c, (bb, p) in enumerate(chains):
            y_ref[bb, pl.ds(base, 8), p * 128:(p + 1) * 128] = y_tiles[c]
            s_ref[c] = st[c]
        return carry

    lax.fori_loop(0, n_grp, group, 0)


def _wkv(r, v, a, dec, km, bb, n_lat_tok, reverse):
    bsz, n_tok, _ = r.shape
    gb = 4 if bsz % 4 == 0 else (2 if bsz % 2 == 0 else 1)
    n_blk = n_tok // WKV_TBLK
    n_lat = n_lat_tok // WKV_TBLK
    n_ctx = n_blk - n_lat
    d = 1 if reverse else 0

    def tb(j):
        return n_blk - 1 - j if reverse else jnp.where(j < n_ctx, n_lat + j, j - n_ctx)

    s1 = pl.BlockSpec((gb, WKV_TBLK, RWKV_DIM), lambda g, j: (g, tb(j), 0))
    s2 = pl.BlockSpec((1, gb, WKV_TBLK, RWKV_DIM), lambda g, j: (d, g, tb(j), 0))
    return pl.pallas_call(
        functools.partial(_wkv_kernel, gb=gb, reverse=reverse),
        grid=(bsz // gb, n_blk),
        in_specs=[s1, s1, s1, s2, s2, s2],
        out_specs=s1,
        out_shape=jax.ShapeDtypeStruct((bsz, n_tok, RWKV_DIM), F32),
        scratch_shapes=[pltpu.VMEM((gb * (RWKV_DIM // 128), HEAD_DIM, 128), F32)],
        compiler_params=_cparams(("parallel", "arbitrary")),
        name="wkv_bwd" if reverse else "wkv_fwd",
    )(r, v, a, dec, km, bb)


def _attn_kernel(lam_ref, qr_ref, qp_ref, kl_ref, kc_ref, vl_ref, vc_ref, o_ref, *, latent):
    lam = lam_ref[0]
    lane = lax.broadcasted_iota(jnp.int32, (1, 128), 1)
    nt = (((1,), (1,)), ((), ()))
    acc_l, acc_c = None, None
    for m in range(2):
        own = (lane // HEAD_DIM) == m
        qp = jnp.where(own, qp_ref[0], jnp.zeros_like(qp_ref[0]))
        sc = lax.dot_general(qp, kc_ref[0], nt, preferred_element_type=F32)
        mx = jnp.max(sc, axis=-1, keepdims=True)
        if latent:
            qr = jnp.where(own, qr_ref[0], jnp.zeros_like(qr_ref[0]))
            sl = lax.dot_general(qr, kl_ref[0], nt, preferred_element_type=F32)
            mx = jnp.maximum(mx, jnp.max(sl, axis=-1, keepdims=True))
            el = jnp.exp(sl - mx)
        ec = jnp.exp(sc - mx)
        den = jnp.sum(ec, axis=-1, keepdims=True)
        if latent:
            den = den + jnp.sum(el, axis=-1, keepdims=True)
        coef = (1.0 / den) * (1.0 if m == 0 else -lam)
        acc_c = ec * coef if acc_c is None else acc_c + ec * coef
        if latent:
            acc_l = el * coef if acc_l is None else acc_l + el * coef
    o = jnp.dot(acc_c.astype(BF16), vc_ref[0], preferred_element_type=F32)
    if latent:
        o = o + jnp.dot(acc_l.astype(BF16), vl_ref[0], preferred_element_type=F32)
    o_ref[0] = o


def _attn(lam, qr, qp, ks, vd, n_lat_tok, latent):
    bsz, n_tok, _ = qr.shape
    n_ctx_tok = n_tok - n_lat_tok
    ctx_blk = n_lat_tok // n_ctx_tok
    if latent:
        qb, n_q, q_off = TM, n_lat_tok // TM, 0
    else:
        qb, n_q, q_off = n_ctx_tok, 1, ctx_blk
    qspec = pl.BlockSpec((1, qb, 128), lambda b, h, i: (b, i + q_off, h))
    lat = pl.BlockSpec((1, n_lat_tok, 128), lambda b, h, i: (b, 0, h))
    ctx = pl.BlockSpec((1, n_ctx_tok, 128), lambda b, h, i: (b, ctx_blk, h))
    return pl.pallas_call(
        functools.partial(_attn_kernel, latent=latent),
        grid=(bsz, DIFF_HEADS, n_q),
        in_specs=[pl.BlockSpec(memory_space=pltpu.SMEM), qspec, qspec, lat, ctx, lat, ctx],
        out_specs=pl.BlockSpec((1, qb, 128), lambda b, h, i: (b, i, h)),
        out_shape=jax.ShapeDtypeStruct((bsz, n_q * qb, DIFF_W), F32),
        compiler_params=_cparams(("parallel", "parallel", "arbitrary")),
        name="attn_lat" if latent else "attn_ctx",
    )(lam, qr, qp, ks, ks, vd, vd)


def _hyb_out_kernel(x_ref, mod_ref, yf_ref, yb_ref, r_ref, v_ref, km_ref, g_ref, ol_ref, oc_ref, ones_ref,
                    rk_ref, lnw_ref, lnb_ref, sg_ref, wo_ref, out_ref, *, n_lat_tiles, sub_scale):
    i = pl.program_id(1)
    ones = ones_ref[...]
    y = yf_ref[0] + yb_ref[0]
    mean = _segsum(y, ones) * (1.0 / HEAD_DIM)
    yc = y - mean
    var = _segsum(yc * yc, ones) * (1.0 / HEAD_DIM)
    yn = yc * lax.rsqrt(var + RWKV_GN_EPS) * lnw_ref[...] + lnb_ref[...]
    r = r_ref[0]
    kb = 0.5 * (km_ref[0, 0] + km_ref[1, 0])
    bonus = _segsum(r * kb * rk_ref[...], ones) * v_ref[0]
    a_mix = ((yn + bonus) * g_ref[0]).astype(BF16)

    o = jnp.where(i < n_lat_tiles, ol_ref[0], oc_ref[0])
    subs = []
    for h in range(DIFF_HEADS):
        oh = o[:, h * DIFF_V:(h + 1) * DIFF_V]
        ms = jnp.mean(oh * oh, axis=-1, keepdims=True)
        subs.append(oh * lax.rsqrt(ms + NORM_EPS) * sg_ref[...] * sub_scale)
    sub = jnp.concatenate(subs, axis=1).astype(BF16)
    mix = (jnp.dot(a_mix, wo_ref[0:RWKV_DIM, :], preferred_element_type=F32)
           + jnp.dot(sub, wo_ref[RWKV_DIM:, :], preferred_element_type=F32))
    out_ref[0] = x_ref[0] + mod_ref[0, 2:3, :] * mix


def _hyb_out(x, mod, yf, yb, r, v, km, g, o_lat, o_ctx, ones64, r_k, ln_w, ln_b, subln_g, w_out,
             n_lat_tiles, sub_scale):
    bsz, n_tok, _ = x.shape
    n_tiles = n_tok // TM
    t1 = lambda w: pl.BlockSpec((1, TM, w), lambda b, i: (b, i, 0))
    t2 = pl.BlockSpec((2, 1, TM, RWKV_DIM), lambda b, i: (0, b, i, 0))
    ol = pl.BlockSpec((1, TM, DIFF_W), lambda b, i: (b, jnp.minimum(i, n_lat_tiles - 1), 0))
    oc = pl.BlockSpec((1, TM, DIFF_W), lambda b, i: (b, 0, 0))
    return pl.pallas_call(
        functools.partial(_hyb_out_kernel, n_lat_tiles=n_lat_tiles, sub_scale=sub_scale),
        grid=(bsz, n_tiles),
        in_specs=[t1(D_MODEL), _mod_spec(n_lat_tiles), t1(RWKV_DIM), t1(RWKV_DIM), t1(RWKV_DIM), t1(RWKV_DIM), t2,
                  t1(RWKV_DIM), ol, oc, _const_spec(ones64.shape), _const_spec(r_k.shape),
                  _const_spec(ln_w.shape), _const_spec(ln_b.shape), _const_spec(subln_g.shape),
                  _const_spec(w_out.shape)],
        out_specs=t1(D_MODEL),
        out_shape=jax.ShapeDtypeStruct(x.shape, F32),
        compiler_params=_cparams(("parallel", "parallel")),
        name="hyb_out",
    )(x, mod, yf, yb, r, v, km, g, o_lat, o_ctx, ones64, r_k, ln_w, ln_b, subln_g, w_out)


def _mlp_kernel(x_ref, mod_ref, g_ref, w1_ref, w2_ref, gf_ref, out_ref, *, final_norm):
    x = x_ref[0]
    h = _rms_mod(x, g_ref[...], mod_ref[0, 3:4, :], mod_ref[0, 4:5, :]).astype(BF16)
    acc = jnp.zeros((TM, D_MODEL), F32)
    for c in range(FFN_DIM // FFN_CHUNK):
        cs = slice(c * FFN_CHUNK, (c + 1) * FFN_CHUNK)
        u = jnp.maximum(jnp.dot(h, w1_ref[:, cs], preferred_element_type=F32), 0.0)
        acc = acc + jnp.dot((u * u).astype(BF16), w2_ref[cs, :], preferred_element_type=F32)
    out = x + mod_ref[0, 5:6, :] * acc
    if final_norm:
        ms = jnp.mean(out * out, axis=-1, keepdims=True)
        out = out * lax.rsqrt(ms + NORM_EPS) * gf_ref[...]
    out_ref[0] = out


def _mlp(x, mod, norm_g, w1, w2, norm_f, n_lat_tiles, final_norm):
    bsz, n_tok, _ = x.shape
    t1 = pl.BlockSpec((1, TM, D_MODEL), lambda b, i: (b, i, 0))
    return pl.pallas_call(
        functools.partial(_mlp_kernel, final_norm=final_norm),
        grid=(bsz, n_tok // TM),
        in_specs=[t1, _mod_spec(n_lat_tiles), _const_spec(norm_g.shape), _const_spec(w1.shape),
                  _const_spec(w2.shape), _const_spec(norm_f.shape)],
        out_specs=t1,
        out_shape=jax.ShapeDtypeStruct(x.shape, F32),
        compiler_params=_cparams(("parallel", "parallel")),
        name="mlp",
    )(x, mod, norm_g, w1, w2, norm_f)


def _ssd_in_kernel(xp_ref, xc_ref, xn_ref, g_ref, mod_ref, wzx_ref, wdt_ref, cw_ref, cb_ref, dtb_ref, alog_ref,
                   z_out, xs_out, xt_out, bm_out, cm_out, cs_out, cst_out, dtt_out, hbuf, pbuf, *, n_lat_tiles):
    _normed_with_halo(xp_ref, xc_ref, xn_ref, g_ref, mod_ref, hbuf, n_lat_tiles)
    hc = hbuf[HALO:HALO + TM, :].astype(BF16)
    z_out[0] = jnp.dot(hc, wzx_ref[:, 0:D_INNER], preferred_element_type=F32)
    dt = jnp.dot(hc, wdt_ref[...], preferred_element_type=F32)
    row = lax.broadcasted_iota(jnp.int32, (CHUNK, CHUNK), 0)
    col = lax.broadcasted_iota(jnp.int32, (CHUNK, CHUNK), 1)
    for d in range(2):
        tri = ((row >= col) if d == 0 else (row <= col)).astype(F32)
        dtd = _softplus(dt + dtb_ref[d])
        dta = dtd * (-jnp.exp(alog_ref[d]))
        for c in range(TM // CHUNK):
            ts = slice(c * CHUNK, (c + 1) * CHUNK)
            cs = jnp.dot(tri, dta[ts], precision=lax.Precision.HIGHEST, preferred_element_type=F32)
            cs_out[d, 0, ts, :] = cs
            cst_out[d, 0, :, ts] = cs.T
            dtt_out[d, 0, :, ts] = dtd[ts].T
    pbuf[...] = jnp.dot(hbuf[...].astype(BF16), wzx_ref[:, D_INNER:], preferred_element_type=F32)
    conv = cb_ref[...] + cw_ref[CONV_W // 2:CONV_W // 2 + 1, :] * pbuf[HALO:HALO + TM, :]
    for j in range(CONV_W):
        if j != CONV_W // 2:
            conv = conv + cw_ref[j:j + 1, :] * pbuf[pl.ds(HALO + j - CONV_W // 2, TM), :]
    act = _silu(conv)
    xs_out[0] = act[:, 0:D_INNER]
    for c in range(D_INNER // 128):
        xt_out[0, c * 128:(c + 1) * 128, :] = act[:, c * 128:(c + 1) * 128].T
    bm_out[0] = act[:, D_INNER:D_INNER + SSD_GROUPS * D_STATE].astype(BF16)
    cm_out[0] = act[:, D_INNER + SSD_GROUPS * D_STATE:].astype(BF16)


def _ssd_in(x, norm_g, mod, w_zx, w_dt, conv_w, conv_b, dt_bias, a_log, n_lat_tiles):
    bsz, n_tok, _ = x.shape
    prev, cur, nxt = _halo_specs(n_lat_tiles, D_MODEL)
    t1 = lambda w: pl.BlockSpec((1, TM, w), lambda b, i: (b, i, 0))
    sh = lambda w, dt: jax.ShapeDtypeStruct((bsz, n_tok, w), dt)
    gw = SSD_GROUPS * D_STATE
    by_time = pl.BlockSpec((2, 1, TM, 128), lambda b, i: (0, b, i, 0))
    by_head = pl.BlockSpec((2, 1, 128, TM), lambda b, i: (0, b, 0, i))
    return pl.pallas_call(
        functools.partial(_ssd_in_kernel, n_lat_tiles=n_lat_tiles),
        grid=(bsz, n_tok // TM),
        in_specs=[prev, cur, nxt, _const_spec((1, D_MODEL)), _mod_spec(n_lat_tiles),
                  _const_spec(w_zx.shape), _const_spec(w_dt.shape), _const_spec(conv_w.shape),
                  _const_spec(conv_b.shape), _const_spec(dt_bias.shape), _const_spec(a_log.shape)],
        out_specs=[t1(D_INNER), t1(D_INNER), pl.BlockSpec((1, D_INNER, TM), lambda b, i: (b, 0, i)),
                   t1(gw), t1(gw), by_time, by_head, by_head],
        out_shape=[sh(D_INNER, F32), sh(D_INNER, F32), jax.ShapeDtypeStruct((bsz, D_INNER, n_tok), F32),
                   sh(gw, BF16), sh(gw, BF16), jax.ShapeDtypeStruct((2, bsz, n_tok, 128), F32),
                   jax.ShapeDtypeStruct((2, bsz, 128, n_tok), F32),
                   jax.ShapeDtypeStruct((2, bsz, 128, n_tok), F32)],
        scratch_shapes=[pltpu.VMEM((TM + 2 * HALO, D_MODEL), F32),
                        pltpu.VMEM((TM + 2 * HALO, CONV_DIM), F32)],
        compiler_params=_cparams(("parallel", "parallel")),
        name="ssd_in",
    )(x, x, x, norm_g, mod, w_zx, w_dt, conv_w, conv_b, dt_bias, a_log)


def _ssd_scan_kernel(xt_ref, bm_ref, cm_ref, cs_ref, cst_ref, dtt_ref, y_ref, h_ref):
    d = pl.program_id(0)
    j = pl.program_id(2)

    @pl.when(j == 0)
    def _():
        h_ref[...] = jnp.zeros_like(h_ref)

    row = lax.broadcasted_iota(jnp.int32, (CHUNK, CHUNK), 0)
    col = lax.broadcasted_iota(jnp.int32, (CHUNK, CHUNK), 1)
    keep = jnp.where(d == 0, row - col, col - row) >= 0
    cs = cs_ref[0, 0]
    cs_t = cst_ref[0, 0]
    dtd_t = dtt_ref[0, 0]
    tot_c = jnp.where(d == 0, cs_t[:, CHUNK - 1:CHUNK], cs_t[:, 0:1])
    coef_t = dtd_t * jnp.exp(tot_c - cs_t)
    e_tot = jnp.exp(jnp.broadcast_to(tot_c, (128, D_STATE)))
    nt = (((1,), (1,)), ((), ()))
    zero = jnp.zeros((HEAD_DIM, 2 * CHUNK), BF16)

    def rows2(t, q):
        return jnp.concatenate([jnp.broadcast_to(t[2 * q + i:2 * q + i + 1, :], (HEAD_DIM, t.shape[1]))
                                for i in range(2)], axis=0)

    for g in range(SSD_GROUPS):
        bc = bm_ref[0, :, g * D_STATE:(g + 1) * D_STATE]
        cc = cm_ref[0, :, g * D_STATE:(g + 1) * D_STATE]
        cb = lax.dot_general(cc, bc, nt, preferred_element_type=F32)
        cc_f = cc.astype(F32)
        for qq in range(SSD_HPG // 2):
            q = g * (SSD_HPG // 2) + qq
            xt = xt_ref[0, q * 128:(q + 1) * 128, :]
            xdt_t = (xt * rows2(dtd_t, q)).astype(BF16)
            xw_t = (xt * rows2(coef_t, q)).astype(BF16)
            state = h_ref[q]
            st_b = state.astype(BF16)
            lhs = []
            for i in range(2):
                h = 2 * q + i
                cs_col = jnp.broadcast_to(cs[:, h:h + 1], (CHUNK, CHUNK))
                lmat = jnp.exp(jnp.where(keep, cs_col - cs_t[h:h + 1, :], -jnp.inf))
                lhs += [(cb * lmat).astype(BF16), (cc_f * jnp.exp(cs_col)).astype(BF16)]
            w_top = jnp.concatenate([xdt_t[0:HEAD_DIM], st_b[0:HEAD_DIM], zero], axis=1)
            w_bot = jnp.concatenate([zero, xdt_t[HEAD_DIM:], st_b[HEAD_DIM:]], axis=1)
            y_ref[0, 0, :, q * 128:(q + 1) * 128] = lax.dot_general(
                jnp.concatenate(lhs, axis=1), jnp.concatenate([w_top, w_bot], axis=0), nt,
                preferred_element_type=F32)
            h_ref[q] = state * rows2(e_tot, q) + jnp.dot(xw_t, bc, preferred_element_type=F32)


def _ssd_scan(xs_t, bm, cm, cs, cs_t, dtd_t, n_lat_tok):
    bsz, _, n_tok = xs_t.shape
    n_blk = n_tok // CHUNK
    n_lat = n_lat_tok // CHUNK
    n_ctx = n_blk - n_lat

    def tb(d, j):
        fwd = jnp.where(j < n_ctx, n_lat + j, j - n_ctx)
        return jnp.where(d == 0, fwd, n_blk - 1 - j)

    t1 = lambda w: pl.BlockSpec((1, CHUNK, w), lambda d, b, j: (b, tb(d, j), 0))
    by_time = pl.BlockSpec((1, 1, CHUNK, 128), lambda d, b, j: (d, b, tb(d, j), 0))
    by_head = pl.BlockSpec((1, 1, 128, CHUNK), lambda d, b, j: (d, b, 0, tb(d, j)))
    return pl.pallas_call(
        _ssd_scan_kernel,
        grid=(2, bsz, n_blk),
        in_specs=[pl.BlockSpec((1, D_INNER, CHUNK), lambda d, b, j: (b, 0, tb(d, j))),
                  t1(SSD_GROUPS * D_STATE), t1(SSD_GROUPS * D_STATE), by_time, by_head, by_head],
        out_specs=pl.BlockSpec((1, 1, CHUNK, D_INNER), lambda d, b, j: (d, b, tb(d, j), 0)),
        out_shape=jax.ShapeDtypeStruct((2, bsz, n_tok, D_INNER), F32),
        scratch_shapes=[pltpu.VMEM((SSD_HEADS // 2, 2 * HEAD_DIM, D_STATE), F32)],
        compiler_params=_cparams(("parallel", "parallel", "arbitrary")),
        name="ssd_scan",
    )(xs_t, bm, cm, cs, cs_t, dtd_t)


def _ssd_out_kernel(x_ref, mod_ref, y_ref, xs_ref, z_ref, dsk_ref, ng_ref, wo_ref, out_ref):
    y = y_ref[0, 0] + y_ref[1, 0] + dsk_ref[...] * xs_ref[0]
    yg = y * _silu(z_ref[0])
    gw = D_INNER // SSD_GROUPS
    parts = []
    for g in range(SSD_GROUPS):
        t = yg[:, g * gw:(g + 1) * gw]
        ms = jnp.mean(t * t, axis=-1, keepdims=True)
        parts.append((t * lax.rsqrt(ms + NORM_EPS) * ng_ref[:, g * gw:(g + 1) * gw]).astype(BF16))
    yn = jnp.concatenate(parts, axis=1)
    out_ref[0] = x_ref[0] + mod_ref[0, 2:3, :] * jnp.dot(yn, wo_ref[...], preferred_element_type=F32)


def _ssd_out(x, mod, y, xs, z, d_skip, norm_g, w_out, n_lat_tiles):
    bsz = x.shape[0]
    t1 = lambda w: pl.BlockSpec((1, TM, w), lambda b, i: (b, i, 0))
    return pl.pallas_call(
        _ssd_out_kernel,
        grid=(bsz, n_lat_tiles),
        in_specs=[t1(D_MODEL), _mod_spec(n_lat_tiles),
                  pl.BlockSpec((2, 1, TM, D_INNER), lambda b, i: (0, b, i, 0)),
                  t1(D_INNER), t1(D_INNER), _const_spec(d_skip.shape), _const_spec(norm_g.shape),
                  _const_spec(w_out.shape)],
        out_specs=t1(D_MODEL),
        out_shape=jax.ShapeDtypeStruct((bsz, n_lat_tiles * TM, D_MODEL), F32),
        compiler_params=_cparams(("parallel", "parallel")),
        name="ssd_out",
    )(x, mod, y, xs, z, d_skip, norm_g, w_out)


def _rope_tables(t_len, n_ctx_tok):
    n_rows = t_len // GRID_W
    rows = jnp.broadcast_to(jnp.arange(n_rows)[:, None], (n_rows, GRID_W)).reshape(-1)
    cols = jnp.broadcast_to(jnp.arange(GRID_W)[None, :], (n_rows, GRID_W)).reshape(-1)
    inv = ROPE_BASE ** (-jnp.arange(0, HEAD_DIM // 2, 2, dtype=F32) / (HEAD_DIM // 2))
    ang_r = rows.astype(F32)[:, None] * inv
    ang_c = cols.astype(F32)[:, None] * inv
    cos = jnp.concatenate([jnp.cos(ang_r)] * 2 + [jnp.cos(ang_c)] * 2, axis=1)
    sin = jnp.concatenate([-jnp.sin(ang_r), jnp.sin(ang_r), -jnp.sin(ang_c), jnp.sin(ang_c)], axis=1)
    reps = DIFF_W // HEAD_DIM
    cos = jnp.concatenate([jnp.tile(cos, (1, reps)), jnp.ones((n_ctx_tok, DIFF_W), F32)], axis=0)
    sin = jnp.concatenate([jnp.tile(sin, (1, reps)), jnp.zeros((n_ctx_tok, DIFF_W), F32)], axis=0)
    return cos, sin


def kernel(x, c, ctx, c_ctx, ada_w, ada_b, norm1_g, norm2_g, mlp_w1, mlp_w2, hy_w_in, hy_w_out, rwkv_mu, rwkv_w0, rwkv_w_up, rwkv_a0, rwkv_a_up, rwkv_g_up, rwkv_k_k, rwkv_k_a, rwkv_r_k, rwkv_ln_w, rwkv_ln_b, diff_lq1, diff_lk1, diff_lq2, diff_lk2, diff_subln_g, ssd_w_in, ssd_conv_w, ssd_conv_b, ssd_dt_bias, ssd_a_log, ssd_d, ssd_norm_g, ssd_w_out, norm_f_g):
    bsz, t_len, _ = x.shape
    l_ctx = ctx.shape[1]
    assert l_ctx == TM and t_len % TM == 0 and t_len % l_ctx == 0
    depth = ada_w.shape[0]
    n_lat_tiles = t_len // TM
    row = lambda t: t.reshape(1, -1)

    cc = jnp.zeros((16, D_MODEL), F32).at[0:bsz].set(c).at[bsz].set(c_ctx)
    mods = _ada(cc, ada_w, ada_b)
    mod_lat = mods[:, 0:bsz]
    mod_ctx = jnp.broadcast_to(mods[:, bsz:bsz + 1], mod_lat.shape)
    mods = jnp.stack([mod_lat, mod_ctx], axis=2).reshape(depth, bsz * 2, 6, D_MODEL)

    xa = jnp.concatenate([x, ctx], axis=1)
    ones64 = (jnp.arange(RWKV_DIM)[:, None] // HEAD_DIM == jnp.arange(RWKV_DIM)[None, :] // HEAD_DIM).astype(BF16)
    cos_t, sin_t = _rope_tables(t_len, l_ctx)

    for li in range(depth):
        last = li == depth - 1
        mod = mods[li]
        if li % 2 == 0:
            e = li // 2
            lam_init = 0.8 - 0.6 * math.exp(-0.3 * li)
            lam = (jnp.exp(jnp.sum(diff_lq1[e] * diff_lk1[e])) - jnp.exp(jnp.sum(diff_lq2[e] * diff_lk2[e]))
                   + lam_init).reshape(1).astype(F32)
            zero = jnp.zeros((64, RWKV_DIM), F32)
            lora = jnp.stack([jnp.concatenate(
                [jnp.concatenate([rwkv_w_up[e, d], zero], axis=1),
                 jnp.concatenate([zero, rwkv_a_up[e, d]], axis=1)], axis=0) for d in range(2)]).astype(BF16)
            (r, v, a, dec, km, bb, g, qr, qp, ks, vd) = _hyb_in(
                xa, row(norm1_g[li]), mod, hy_w_in[e].astype(BF16), row(rwkv_mu[e]), row(rwkv_k_k[e]),
                row(rwkv_k_a[e]), rwkv_w0[e], rwkv_a0[e], lora, rwkv_g_up[e].astype(BF16), ones64,
                cos_t, sin_t, n_lat_tiles)
            yf = _wkv(r, v, a, dec, km, bb, t_len, False)
            yb = _wkv(r, v, a, dec, km, bb, t_len, True)
            o_lat = _attn(lam, qr, qp, ks, vd, t_len, True)
            o_ctx = _attn(lam, qr, qp, ks, vd, t_len, False)
            xa = _hyb_out(xa, mod, yf, yb, r, v, km, g, o_lat, o_ctx, ones64, row(rwkv_r_k[e]),
                          row(rwkv_ln_w[e]), row(rwkv_ln_b[e]),
                          row(diff_subln_g[e]), hy_w_out[e].astype(BF16), n_lat_tiles, 1.0 - lam_init)
        else:
            o = li // 2
            w_in = ssd_w_in[o]
            w_zx = w_in[:, 0:D_INNER + CONV_DIM].astype(BF16)
            w_dt = jnp.pad(w_in[:, D_INNER + CONV_DIM:], ((0, 0), (0, 128 - SSD_HEADS))).astype(BF16)
            pad_h = lambda t: jnp.pad(t, ((0, 0), (0, 128 - SSD_HEADS))).reshape(2, 1, 128)
            z, xs, xs_t, bm, cm, cs, cs_t, dtd_t = _ssd_in(
                xa, row(norm1_g[li]), mod, w_zx, w_dt, ssd_conv_w[o], row(ssd_conv_b[o]),
                pad_h(ssd_dt_bias[o]), pad_h(ssd_a_log[o]), n_lat_tiles)
            y = _ssd_scan(xs_t, bm, cm, cs, cs_t, dtd_t, t_len)
            d_skip = row(jnp.repeat(ssd_d[o], HEAD_DIM))
            if last:
                xa = _ssd_out(xa, mod, y, xs, z, d_skip, row(ssd_norm_g[o]), ssd_w_out[o].astype(BF16),
                              n_lat_tiles)
            else:
                raise NotImplementedError("an SSD layer that is not the last layer")
        xa = _mlp(xa, mod, row(norm2_g[li]), mlp_w1[li].astype(BF16), mlp_w2[li].astype(BF16),
                  row(norm_f_g), n_lat_tiles, last)
    return xa[:, 0:t_len]
```

```python
import functools
import math

import jax
import jax.numpy as jnp
from jax import lax
from jax.experimental import pallas as pl
from jax.experimental.pallas import tpu as pltpu

F32 = jnp.float32
BF16 = jnp.bfloat16

D_MODEL = 1024
FFN_DIM = 4 * D_MODEL
NORM_EPS = 1e-6
GRID_W = 64
ROPE_BASE = 10000.0
LOG2_E = math.log2(math.e)

HEAD_DIM = 64
RWKV_DIM = 512
RWKV_HEADS = 8
RWKV_COLS = 1792
RWKV_GN_EPS = 64e-5
DIFF_HEADS = 4
DIFF_V = 128
DIFF_W = 512
HYB_IN = 3328

D_INNER = 2048
SSD_HEADS = 32
SSD_GROUPS = 8
SSD_HPG = 4
D_STATE = 128
CONV_W = 5
CONV_DIM = 4096
CHUNK = 128

TM = 256
HALO = 8
FFN_CHUNK = 1024
WKV_TBLK = 128
WKV_SUB = 8
VMEM_LIMIT = 56 * 1024 * 1024


def _cparams(sem):
    return pltpu.CompilerParams(dimension_semantics=sem, vmem_limit_bytes=VMEM_LIMIT)


def _const_spec(shape):
    nd = len(shape)
    return pl.BlockSpec(shape, lambda *_: (0,) * nd, pipeline_mode=pl.Buffered(1))


def _split2(x):
    hi = x.astype(BF16)
    lo = (x - hi.astype(F32)).astype(BF16)
    return hi, lo


def _segsum(x, ones):
    hi, lo = _split2(x)
    return (jnp.dot(hi, ones, preferred_element_type=F32)
            + jnp.dot(lo, ones, preferred_element_type=F32))


def _rms_mod(x, g, shift, scale):
    ms = jnp.mean(x * x, axis=-1, keepdims=True)
    return (x * lax.rsqrt(ms + NORM_EPS) * g) * (1.0 + scale) + shift


def _sigmoid(x):
    return 1.0 / (1.0 + jnp.exp(-x))


def _softplus(x):
    return jnp.maximum(x, 0.0) + jnp.log(1.0 + jnp.exp(-jnp.abs(x)))


def _silu(x):
    return x * _sigmoid(x)


def _ada_kernel(c_ref, w_ref, b_ref, o_ref):
    o_ref[0] = jnp.dot(_silu(c_ref[...]), w_ref[0], precision=lax.Precision.HIGHEST,
                       preferred_element_type=F32) + b_ref[0]


def _ada(cc, ada_w, ada_b):
    depth, d, n = ada_w.shape
    tn = 1536
    return pl.pallas_call(
        _ada_kernel,
        grid=(depth, n // tn),
        in_specs=[_const_spec(cc.shape),
                  pl.BlockSpec((1, d, tn), lambda l, j: (l, 0, j)),
                  pl.BlockSpec((1, 1, tn), lambda l, j: (l, 0, j))],
        out_specs=pl.BlockSpec((1, cc.shape[0], tn), lambda l, j: (l, 0, j)),
        out_shape=jax.ShapeDtypeStruct((depth, cc.shape[0], n), F32),
        compiler_params=_cparams(("parallel", "parallel")),
        name="ada",
    )(cc, ada_w, ada_b.reshape(depth, 1, n))


def _halo_specs(n_lat_tiles, width):
    per = TM // HALO
    last = (n_lat_tiles + 1) * per - 1
    prev = pl.BlockSpec((1, HALO, width), lambda b, i: (b, jnp.maximum(i * per - 1, 0), 0))
    cur = pl.BlockSpec((1, TM, width), lambda b, i: (b, i, 0))
    nxt = pl.BlockSpec((1, HALO, width), lambda b, i: (b, jnp.minimum((i + 1) * per, last), 0))
    return prev, cur, nxt


def _mod_spec(n_lat_tiles):
    return pl.BlockSpec((1, 6, D_MODEL), lambda b, i: (b * 2 + (i >= n_lat_tiles).astype(jnp.int32), 0, 0))


def _normed_with_halo(xp_ref, xc_ref, xn_ref, g_ref, mod_ref, hbuf, n_lat_tiles):
    i = pl.program_id(1)
    g = g_ref[...]
    shift, scale = mod_ref[0, 0:1, :], mod_ref[0, 1:2, :]
    prev_ok = jnp.logical_and(i >= 1, i < n_lat_tiles).astype(F32)
    next_ok = (i < n_lat_tiles - 1).astype(F32)
    hbuf[0:HALO, :] = _rms_mod(xp_ref[0], g, shift, scale) * prev_ok
    hbuf[HALO:HALO + TM, :] = _rms_mod(xc_ref[0], g, shift, scale)
    hbuf[HALO + TM:, :] = _rms_mod(xn_ref[0], g, shift, scale) * next_ok


def _hyb_in_kernel(xp_ref, xc_ref, xn_ref, g_ref, mod_ref, w_ref, mu_ref, kk_w_ref, ka_ref,
                   w0_ref, a0_ref, lora_ref, gup_ref, ones_ref, cos_ref, sin_ref,
                   r_out, v_out, a_out, dec_out, km_out, bb_out, g_out,
                   qr_out, qp_out, ks_out, vd_out, hbuf, pbuf, *, n_lat_tiles):
    _normed_with_halo(xp_ref, xc_ref, xn_ref, g_ref, mod_ref, hbuf, n_lat_tiles)
    pbuf[...] = jnp.dot(hbuf[...].astype(BF16), w_ref[...], preferred_element_type=F32)

    p = pbuf[HALO:HALO + TM, 0:RWKV_COLS]
    prev = pbuf[pl.ds(HALO - 1, TM), 0:RWKV_COLS]
    nxt = pbuf[pl.ds(HALO + 1, TM), 0:RWKV_COLS]
    p = p + (0.5 * (prev + nxt) - p) * mu_ref[...]
    r = p[:, 0:RWKV_DIM]
    k = p[:, RWKV_DIM:2 * RWKV_DIM]
    v = p[:, 2 * RWKV_DIM:3 * RWKV_DIM]
    lo_in = p[:, 3 * RWKV_DIM:3 * RWKV_DIM + 128]
    gd = p[:, 3 * RWKV_DIM + 128:RWKV_COLS]
    kkf = k * kk_w_ref[...]
    kk = kkf * lax.rsqrt(_segsum(kkf * kkf, ones_ref[...]) + 1e-12)
    r_out[0] = r
    v_out[0] = v
    a_out[0] = -kk
    g_out[0] = jnp.dot(_sigmoid(gd).astype(BF16), gup_ref[...], preferred_element_type=F32)
    lane = lax.broadcasted_iota(jnp.int32, lo_in.shape, 1)
    lo_act = jnp.where(lane < 64, jnp.tanh(lo_in), lo_in).astype(BF16)
    for d in range(2):
        up = jnp.dot(lo_act, lora_ref[d], preferred_element_type=F32)
        wlog = -_softplus(-(w0_ref[d:d + 1, :] + up[:, 0:RWKV_DIM])) - 0.5
        dec_out[d, 0] = jnp.exp(-jnp.exp(wlog))
        a = _sigmoid(a0_ref[d:d + 1, :] + up[:, RWKV_DIM:])
        km_out[d, 0] = k * (1.0 + (a - 1.0) * ka_ref[...])
        bb_out[d, 0] = kk * a

    o = RWKV_COLS
    vd_out[0] = pbuf[HALO:HALO + TM, o + 2 * DIFF_W:o + 3 * DIFF_W].astype(BF16)
    lane = lax.broadcasted_iota(jnp.int32, (TM, 128), 1)
    first = (lane % 32) < 16
    for s in range(DIFF_W // 128):
        ls = slice(s * 128, (s + 1) * 128)
        cos, sin = cos_ref[:, ls], sin_ref[:, ls]

        def rope(t):
            partner = jnp.where(first, pltpu.roll(t, 128 - 16, 1), pltpu.roll(t, 16, 1))
            return t * cos + partner * sin

        q = pbuf[HALO:HALO + TM, o + s * 128:o + (s + 1) * 128] * (HEAD_DIM ** -0.5 * LOG2_E)
        kd = pbuf[HALO:HALO + TM, o + DIFF_W + s * 128:o + DIFF_W + (s + 1) * 128]
        qp_out[0, :, ls] = q.astype(BF16)
        qr_out[0, :, ls] = rope(q).astype(BF16)
        ks_out[0, :, ls] = rope(kd).astype(BF16)


def _hyb_in(x, norm_g, mod, w_in, mu, k_k, k_a, w0, a0, lora, g_up, ones64, cos_t, sin_t, n_lat_tiles):
    bsz, n_tok, _ = x.shape
    n_tiles = n_tok // TM
    prev, cur, nxt = _halo_specs(n_lat_tiles, D_MODEL)
    tile = lambda w, dt: jax.ShapeDtypeStruct((bsz, n_tok, w), dt)
    tile2 = lambda w, dt: jax.ShapeDtypeStruct((2, bsz, n_tok, w), dt)
    o1 = pl.BlockSpec((1, TM, RWKV_DIM), lambda b, i: (b, i, 0))
    o2 = pl.BlockSpec((2, 1, TM, RWKV_DIM), lambda b, i: (0, b, i, 0))
    tab = pl.BlockSpec((TM, DIFF_W), lambda b, i: (i, 0))
    return pl.pallas_call(
        functools.partial(_hyb_in_kernel, n_lat_tiles=n_lat_tiles),
        grid=(bsz, n_tiles),
        in_specs=[prev, cur, nxt, _const_spec((1, D_MODEL)), _mod_spec(n_lat_tiles),
                  _const_spec(w_in.shape), _const_spec(mu.shape), _const_spec(k_k.shape),
                  _const_spec(k_a.shape), _const_spec(w0.shape), _const_spec(a0.shape),
                  _const_spec(lora.shape), _const_spec(g_up.shape), _const_spec(ones64.shape), tab, tab],
        out_specs=[o1, o1, o1, o2, o2, o2, o1, o1, o1, o1, o1],
        out_shape=[tile(RWKV_DIM, F32)] * 3 + [tile2(RWKV_DIM, F32)] * 3 + [tile(RWKV_DIM, F32)]
                  + [tile(DIFF_W, BF16)] * 4,
        scratch_shapes=[pltpu.VMEM((TM + 2 * HALO, D_MODEL), F32),
                        pltpu.VMEM((TM + 2 * HALO, HYB_IN), F32)],
        compiler_params=_cparams(("parallel", "parallel")),
        name="hyb_in",
    )(x, x, x, norm_g, mod, w_in, mu, k_k, k_a, w0, a0, lora, g_up, ones64, cos_t, sin_t)


def _wkv_kernel(r_ref, v_ref, a_ref, w_ref, k_ref, b_ref, y_ref, s_ref, *, gb, reverse):
    @pl.when(pl.program_id(1) == 0)
    def _():
        s_ref[...] = jnp.zeros_like(s_ref)

    lane = lax.broadcasted_iota(jnp.int32, (HEAD_DIM, 128), 1)
    sub = lax.broadcasted_iota(jnp.int32, (HEAD_DIM, 128), 0)
    sel = (lane % HEAD_DIM == sub).astype(F32)
    row = lax.broadcasted_iota(jnp.int32, (256, 128), 0)
    col = lax.broadcasted_iota(jnp.int32, (256, 128), 1)
    ones2 = ((row % 128) // HEAD_DIM == col // HEAD_DIM).astype(BF16)
    sub8 = lax.broadcasted_iota(jnp.int32, (8, 128), 0)
    sel_b = sel.astype(BF16)
    row = lax.broadcasted_iota(jnp.int32, (256, 256), 0)
    col = lax.broadcasted_iota(jnp.int32, (256, 256), 1)
    ones_bd = (row // HEAD_DIM == col // HEAD_DIM).astype(BF16)
    n_grp = WKV_TBLK // 8
    n_pair = RWKV_DIM // 128
    chains = [(bb, p) for bb in range(gb) for p in range(n_pair)]
    n_ch = len(chains)
    n_sub = min(n_ch, WKV_SUB)
    half = n_sub // 2

    def hilo(x):
        hi, lo = _split2(x)
        return jnp.concatenate([hi, lo], axis=1)

    def pairs(xs):
        return [jnp.concatenate([xs[i], xs[i + half]], axis=1) for i in range(half)]

    def group(i, carry):
        base = pl.multiple_of((n_grp - 1 - i if reverse else i) * 8, 8)
        tiles = []
        for bb, p in chains:
            sl = slice(p * 128, (p + 1) * 128)
            tiles.append([ref[bb, pl.ds(base, 8), sl] for ref in (a_ref, v_ref, r_ref)]
                         + [ref[0, bb, pl.ds(base, 8), sl] for ref in (w_ref, k_ref, b_ref)])
        y_tiles = [jnp.zeros((8, 128), F32) for _ in chains]
        st = [s_ref[c] for c in range(n_ch)]
        dots = jnp.dot(jnp.concatenate([hilo(tl[5] * tl[2]) for tl in tiles]
                                       + [hilo(tl[4] * tl[2]) for tl in tiles], axis=0),
                       ones2, preferred_element_type=F32)
        wr = [tl[3] * tl[2] + tl[0] * dots[c * 8:(c + 1) * 8] for c, tl in enumerate(tiles)]
        kr = [dots[(n_ch + c) * 8:(n_ch + c + 1) * 8] for c in range(n_ch)]
        for tt in (reversed(range(8)) if reverse else range(8)):
            rows = [[t[tt:tt + 1, :] for t in tl] for tl in tiles]
            for sub_grp in range(0, n_ch, n_sub):
                cs = list(range(sub_grp, sub_grp + n_sub))
                lhs = pairs([(st[c] * rows[c][0]).astype(BF16) for c in cs])
                lhs += pairs([sel_b * rows[c][1].astype(BF16) for c in cs])
                lhs += pairs([(st[c] * wr[c][tt:tt + 1, :]).astype(BF16) for c in cs])
                red = jnp.dot(jnp.concatenate(lhs, axis=0), ones_bd, preferred_element_type=F32)
                for i, c in enumerate(cs):
                    _, v_t, _, w_t, k_t, b_t = rows[c]
                    j, side = i % half, slice((i // half) * 128, (i // half + 1) * 128)
                    sa = red[j * HEAD_DIM:(j + 1) * HEAD_DIM, side]
                    vb = red[(half + j) * HEAD_DIM:(half + j + 1) * HEAD_DIM, side]
                    rz = red[(2 * half + j) * HEAD_DIM:(2 * half + j + 1) * HEAD_DIM, side]
                    st[c] = st[c] * w_t + sa * b_t + vb * k_t
                    y_row = jnp.sum(rz * sel, axis=0, keepdims=True) + v_t * kr[c][tt:tt + 1, :]
                    y_tiles[c] = jnp.where(sub8 == tt, y_row, y_tiles[c])
        for c, (bb, p) in enumerate(chains):
            y_ref[bb, pl.ds(base, 8), p * 128:(p + 1) * 128] = y_tiles[c]
            s_ref[c] = st[c]
        return carry

    lax.fori_loop(0, n_grp, group, 0)


def _wkv(r, v, a, dec, km, bb, n_lat_tok, reverse):
    bsz, n_tok, _ = r.shape
    gb = 4 if bsz % 4 == 0 else (2 if bsz % 2 == 0 else 1)
    n_blk = n_tok // WKV_TBLK
    n_lat = n_lat_tok // WKV_TBLK
    n_ctx = n_blk - n_lat
    d = 1 if reverse else 0

    def tb(j):
        return n_blk - 1 - j if reverse else jnp.where(j < n_ctx, n_lat + j, j - n_ctx)

    s1 = pl.BlockSpec((gb, WKV_TBLK, RWKV_DIM), lambda g, j: (g, tb(j), 0))
    s2 = pl.BlockSpec((1, gb, WKV_TBLK, RWKV_DIM), lambda g, j: (d, g, tb(j), 0))
    return pl.pallas_call(
        functools.partial(_wkv_kernel, gb=gb, reverse=reverse),
        grid=(bsz // gb, n_blk),
        in_specs=[s1, s1, s1, s2, s2, s2],
        out_specs=s1,
        out_shape=jax.ShapeDtypeStruct((bsz, n_tok, RWKV_DIM), F32),
        scratch_shapes=[pltpu.VMEM((gb * (RWKV_DIM // 128), HEAD_DIM, 128), F32)],
        compiler_params=_cparams(("parallel", "arbitrary")),
        name="wkv_bwd" if reverse else "wkv_fwd",
    )(r, v, a, dec, km, bb)


def _attn_kernel(lam_ref, qr_ref, qp_ref, kl_ref, kc_ref, vl_ref, vc_ref, o_ref, *, latent):
    lam = lam_ref[0]
    lane = lax.broadcasted_iota(jnp.int32, (1, 128), 1)
    nt = (((1,), (1,)), ((), ()))
    acc_l, acc_c, den0 = None, None, None
    for m in range(2):
        own = (lane // HEAD_DIM) == m
        qp = jnp.where(own, qp_ref[0], jnp.zeros_like(qp_ref[0]))
        sc = lax.dot_general(qp, kc_ref[0], nt, preferred_element_type=F32)
        mx = jnp.max(sc, axis=-1, keepdims=True)
        if latent:
            qr = jnp.where(own, qr_ref[0], jnp.zeros_like(qr_ref[0]))
            sl = lax.dot_general(qr, kl_ref[0], nt, preferred_element_type=F32)
            mx = jnp.maximum(mx, jnp.max(sl, axis=-1, keepdims=True))
            el = jnp.exp2(sl - mx)
        ec = jnp.exp2(sc - mx)
        den = jnp.sum(ec, axis=-1, keepdims=True)
        if latent:
            den = den + jnp.sum(el, axis=-1, keepdims=True)
        if m == 0:
            den0, acc_c, acc_l = den, ec, (el if latent else None)
        else:
            ratio = -lam * den0 / den
            acc_c = acc_c + ec * ratio
            if latent:
                acc_l = acc_l + el * ratio
    o = jnp.dot(acc_c.astype(BF16), vc_ref[0], preferred_element_type=F32)
    if latent:
        o = o + jnp.dot(acc_l.astype(BF16), vl_ref[0], preferred_element_type=F32)
    o_ref[0] = o / den0


def _attn(lam, qr, qp, ks, vd, n_lat_tok, latent):
    bsz, n_tok, _ = qr.shape
    n_ctx_tok = n_tok - n_lat_tok
    ctx_blk = n_lat_tok // n_ctx_tok
    if latent:
        qb, n_q, q_off = TM, n_lat_tok // TM, 0
    else:
        qb, n_q, q_off = n_ctx_tok, 1, ctx_blk
    qspec = pl.BlockSpec((1, qb, 128), lambda b, h, i: (b, i + q_off, h))
    lat = pl.BlockSpec((1, n_lat_tok, 128), lambda b, h, i: (b, 0, h))
    ctx = pl.BlockSpec((1, n_ctx_tok, 128), lambda b, h, i: (b, ctx_blk, h))
    return pl.pallas_call(
        functools.partial(_attn_kernel, latent=latent),
        grid=(bsz, DIFF_HEADS, n_q),
        in_specs=[pl.BlockSpec(memory_space=pltpu.SMEM), qspec, qspec, lat, ctx, lat, ctx],
        out_specs=pl.BlockSpec((1, qb, 128), lambda b, h, i: (b, i, h)),
        out_shape=jax.ShapeDtypeStruct((bsz, n_q * qb, DIFF_W), F32),
        compiler_params=_cparams(("parallel", "parallel", "arbitrary")),
        name="attn_lat" if latent else "attn_ctx",
    )(lam, qr, qp, ks, ks, vd, vd)


def _hyb_out_kernel(x_ref, mod_ref, yf_ref, yb_ref, r_ref, v_ref, km_ref, g_ref, ol_ref, oc_ref, ones_ref,
                    rk_ref, lnw_ref, lnb_ref, sg_ref, wo_ref, out_ref, *, n_lat_tiles, sub_scale):
    i = pl.program_id(1)
    ones = ones_ref[...]
    y = yf_ref[0] + yb_ref[0]
    mean = _segsum(y, ones) * (1.0 / HEAD_DIM)
    yc = y - mean
    var = _segsum(yc * yc, ones) * (1.0 / HEAD_DIM)
    yn = yc * lax.rsqrt(var + RWKV_GN_EPS) * lnw_ref[...] + lnb_ref[...]
    r = r_ref[0]
    kb = 0.5 * (km_ref[0, 0] + km_ref[1, 0])
    bonus = _segsum(r * kb * rk_ref[...], ones) * v_ref[0]
    a_mix = ((yn + bonus) * g_ref[0]).astype(BF16)

    o = jnp.where(i < n_lat_tiles, ol_ref[0], oc_ref[0])
    subs = []
    for h in range(DIFF_HEADS):
        oh = o[:, h * DIFF_V:(h + 1) * DIFF_V]
        ms = jnp.mean(oh * oh, axis=-1, keepdims=True)
        subs.append(oh * lax.rsqrt(ms + NORM_EPS) * sg_ref[...] * sub_scale)
    sub = jnp.concatenate(subs, axis=1).astype(BF16)
    mix = (jnp.dot(a_mix, wo_ref[0:RWKV_DIM, :], preferred_element_type=F32)
           + jnp.dot(sub, wo_ref[RWKV_DIM:, :], preferred_element_type=F32))
    out_ref[0] = x_ref[0] + mod_ref[0, 2:3, :] * mix


def _hyb_out(x, mod, yf, yb, r, v, km, g, o_lat, o_ctx, ones64, r_k, ln_w, ln_b, subln_g, w_out,
             n_lat_tiles, sub_scale):
    bsz, n_tok, _ = x.shape
    n_tiles = n_tok // TM
    t1 = lambda w: pl.BlockSpec((1, TM, w), lambda b, i: (b, i, 0))
    t2 = pl.BlockSpec((2, 1, TM, RWKV_DIM), lambda b, i: (0, b, i, 0))
    ol = pl.BlockSpec((1, TM, DIFF_W), lambda b, i: (b, jnp.minimum(i, n_lat_tiles - 1), 0))
    oc = pl.BlockSpec((1, TM, DIFF_W), lambda b, i: (b, 0, 0))
    return pl.pallas_call(
        functools.partial(_hyb_out_kernel, n_lat_tiles=n_lat_tiles, sub_scale=sub_scale),
        grid=(bsz, n_tiles),
        in_specs=[t1(D_MODEL), _mod_spec(n_lat_tiles), t1(RWKV_DIM), t1(RWKV_DIM), t1(RWKV_DIM), t1(RWKV_DIM), t2,
                  t1(RWKV_DIM), ol, oc, _const_spec(ones64.shape), _const_spec(r_k.shape),
                  _const_spec(ln_w.shape), _const_spec(ln_b.shape), _const_spec(subln_g.shape),
                  _const_spec(w_out.shape)],
        out_specs=t1(D_MODEL),
        out_shape=jax.ShapeDtypeStruct(x.shape, F32),
        compiler_params=_cparams(("parallel", "parallel")),
        name="hyb_out",
    )(x, mod, yf, yb, r, v, km, g, o_lat, o_ctx, ones64, r_k, ln_w, ln_b, subln_g, w_out)


def _mlp_kernel(x_ref, mod_ref, g_ref, w1_ref, w2_ref, gf_ref, out_ref, *, final_norm):
    x = x_ref[0]
    h = _rms_mod(x, g_ref[...], mod_ref[0, 3:4, :], mod_ref[0, 4:5, :]).astype(BF16)
    acc = jnp.zeros((TM, D_MODEL), F32)
    for c in range(FFN_DIM // FFN_CHUNK):
        cs = slice(c * FFN_CHUNK, (c + 1) * FFN_CHUNK)
        u = jnp.maximum(jnp.dot(h, w1_ref[:, cs], preferred_element_type=F32), 0.0)
        acc = acc + jnp.dot((u * u).astype(BF16), w2_ref[cs, :], preferred_element_type=F32)
    out = x + mod_ref[0, 5:6, :] * acc
    if final_norm:
        ms = jnp.mean(out * out, axis=-1, keepdims=True)
        out = out * lax.rsqrt(ms + NORM_EPS) * gf_ref[...]
    out_ref[0] = out


def _mlp(x, mod, norm_g, w1, w2, norm_f, n_lat_tiles, final_norm):
    bsz, n_tok, _ = x.shape
    t1 = pl.BlockSpec((1, TM, D_MODEL), lambda b, i: (b, i, 0))
    return pl.pallas_call(
        functools.partial(_mlp_kernel, final_norm=final_norm),
        grid=(bsz, n_tok // TM),
        in_specs=[t1, _mod_spec(n_lat_tiles), _const_spec(norm_g.shape), _const_spec(w1.shape),
                  _const_spec(w2.shape), _const_spec(norm_f.shape)],
        out_specs=t1,
        out_shape=jax.ShapeDtypeStruct(x.shape, F32),
        compiler_params=_cparams(("parallel", "parallel")),
        name="mlp",
    )(x, mod, norm_g, w1, w2, norm_f)


def _ssd_in_kernel(xp_ref, xc_ref, xn_ref, g_ref, mod_ref, wzx_ref, wdt_ref, cw_ref, cb_ref, dtb_ref, alog_ref,
                   z_out, xs_out, xt_out, bm_out, cm_out, cs_out, cst_out, dtt_out, hbuf, pbuf, *, n_lat_tiles):
    _normed_with_halo(xp_ref, xc_ref, xn_ref, g_ref, mod_ref, hbuf, n_lat_tiles)
    hc = hbuf[HALO:HALO + TM, :].astype(BF16)
    z_out[0] = jnp.dot(hc, wzx_ref[:, 0:D_INNER], preferred_element_type=F32)
    dt = jnp.dot(hc, wdt_ref[...], preferred_element_type=F32)
    row = lax.broadcasted_iota(jnp.int32, (CHUNK, CHUNK), 0)
    col = lax.broadcasted_iota(jnp.int32, (CHUNK, CHUNK), 1)
    for d in range(2):
        tri = ((row >= col) if d == 0 else (row <= col)).astype(F32)
        dtd = _softplus(dt + dtb_ref[d])
        dta = dtd * (-jnp.exp(alog_ref[d]))
        for c in range(TM // CHUNK):
            ts = slice(c * CHUNK, (c + 1) * CHUNK)
            cs = jnp.dot(tri, dta[ts], precision=lax.Precision.HIGHEST, preferred_element_type=F32)
            cs_out[d, 0, ts, :] = cs
            cst_out[d, 0, :, ts] = cs.T
            dtt_out[d, 0, :, ts] = dtd[ts].T
    pbuf[...] = jnp.dot(hbuf[...].astype(BF16), wzx_ref[:, D_INNER:], preferred_element_type=F32)
    conv = cb_ref[...] + cw_ref[CONV_W // 2:CONV_W // 2 + 1, :] * pbuf[HALO:HALO + TM, :]
    for j in range(CONV_W):
        if j != CONV_W // 2:
            conv = conv + cw_ref[j:j + 1, :] * pbuf[pl.ds(HALO + j - CONV_W // 2, TM), :]
    act = _silu(conv)
    xs_out[0] = act[:, 0:D_INNER]
    for c in range(D_INNER // 128):
        xt_out[0, c * 128:(c + 1) * 128, :] = act[:, c * 128:(c + 1) * 128].T
    bm_out[0] = act[:, D_INNER:D_INNER + SSD_GROUPS * D_STATE].astype(BF16)
    cm_out[0] = act[:, D_INNER + SSD_GROUPS * D_STATE:].astype(BF16)


def _ssd_in(x, norm_g, mod, w_zx, w_dt, conv_w, conv_b, dt_bias, a_log, n_lat_tiles):
    bsz, n_tok, _ = x.shape
    prev, cur, nxt = _halo_specs(n_lat_tiles, D_MODEL)
    t1 = lambda w: pl.BlockSpec((1, TM, w), lambda b, i: (b, i, 0))
    sh = lambda w, dt: jax.ShapeDtypeStruct((bsz, n_tok, w), dt)
    gw = SSD_GROUPS * D_STATE
    by_time = pl.BlockSpec((2, 1, TM, 128), lambda b, i: (0, b, i, 0))
    by_head = pl.BlockSpec((2, 1, 128, TM), lambda b, i: (0, b, 0, i))
    return pl.pallas_call(
        functools.partial(_ssd_in_kernel, n_lat_tiles=n_lat_tiles),
        grid=(bsz, n_tok // TM),
        in_specs=[prev, cur, nxt, _const_spec((1, D_MODEL)), _mod_spec(n_lat_tiles),
                  _const_spec(w_zx.shape), _const_spec(w_dt.shape), _const_spec(conv_w.shape),
                  _const_spec(conv_b.shape), _const_spec(dt_bias.shape), _const_spec(a_log.shape)],
        out_specs=[t1(D_INNER), t1(D_INNER), pl.BlockSpec((1, D_INNER, TM), lambda b, i: (b, 0, i)),
                   t1(gw), t1(gw), by_time, by_head, by_head],
        out_shape=[sh(D_INNER, F32), sh(D_INNER, F32), jax.ShapeDtypeStruct((bsz, D_INNER, n_tok), F32),
                   sh(gw, BF16), sh(gw, BF16), jax.ShapeDtypeStruct((2, bsz, n_tok, 128), F32),
                   jax.ShapeDtypeStruct((2, bsz, 128, n_tok), F32),
                   jax.ShapeDtypeStruct((2, bsz, 128, n_tok), F32)],
        scratch_shapes=[pltpu.VMEM((TM + 2 * HALO, D_MODEL), F32),
                        pltpu.VMEM((TM + 2 * HALO, CONV_DIM), F32)],
        compiler_params=_cparams(("parallel", "parallel")),
        name="ssd_in",
    )(x, x, x, norm_g, mod, w_zx, w_dt, conv_w, conv_b, dt_bias, a_log)


def _ssd_scan_kernel(xt_ref, bm_ref, cm_ref, cs_ref, cst_ref, dtt_ref, y_ref, h_ref):
    d = pl.program_id(0)
    j = pl.program_id(2)

    @pl.when(j == 0)
    def _():
        h_ref[...] = jnp.zeros_like(h_ref)

    row = lax.broadcasted_iota(jnp.int32, (CHUNK, CHUNK), 0)
    col = lax.broadcasted_iota(jnp.int32, (CHUNK, CHUNK), 1)
    keep = jnp.where(d == 0, row - col, col - row) >= 0
    cs = cs_ref[0, 0]
    cs_t = cst_ref[0, 0]
    dtd_t = dtt_ref[0, 0]
    tot_c = jnp.where(d == 0, cs_t[:, CHUNK - 1:CHUNK], cs_t[:, 0:1])
    coef_t = dtd_t * jnp.exp(tot_c - cs_t)
    e_tot = jnp.exp(jnp.broadcast_to(tot_c, (128, D_STATE)))
    nt = (((1,), (1,)), ((), ()))
    zero = jnp.zeros((HEAD_DIM, 2 * CHUNK), BF16)

    def rows2(t, q):
        return jnp.concatenate([jnp.broadcast_to(t[2 * q + i:2 * q + i + 1, :], (HEAD_DIM, t.shape[1]))
                                for i in range(2)], axis=0)

    for g in range(SSD_GROUPS):
        bc = bm_ref[0, :, g * D_STATE:(g + 1) * D_STATE]
        cc = cm_ref[0, :, g * D_STATE:(g + 1) * D_STATE]
        cb = lax.dot_general(cc, bc, nt, preferred_element_type=F32)
        cc_f = cc.astype(F32)
        for qq in range(SSD_HPG // 2):
            q = g * (SSD_HPG // 2) + qq
            xt = xt_ref[0, q * 128:(q + 1) * 128, :]
            xdt_t = (xt * rows2(dtd_t, q)).astype(BF16)
            xw_t = (xt * rows2(coef_t, q)).astype(BF16)
            state = h_ref[q]
            st_b = state.astype(BF16)
            lhs = []
            for i in range(2):
                h = 2 * q + i
                cs_col = jnp.broadcast_to(cs[:, h:h + 1], (CHUNK, CHUNK))
                lmat = jnp.exp(jnp.where(keep, cs_col - cs_t[h:h + 1, :], -jnp.inf))
                lhs += [(cb * lmat).astype(BF16), (cc_f * jnp.exp(cs_col)).astype(BF16)]
            w_top = jnp.concatenate([xdt_t[0:HEAD_DIM], st_b[0:HEAD_DIM], zero], axis=1)
            w_bot = jnp.concatenate([zero, xdt_t[HEAD_DIM:], st_b[HEAD_DIM:]], axis=1)
            y_ref[0, 0, :, q * 128:(q + 1) * 128] = lax.dot_general(
                jnp.concatenate(lhs, axis=1), jnp.concatenate([w_top, w_bot], axis=0), nt,
                preferred_element_type=F32)
            h_ref[q] = state * rows2(e_tot, q) + jnp.dot(xw_t, bc, preferred_element_type=F32)


def _ssd_scan(xs_t, bm, cm, cs, cs_t, dtd_t, n_lat_tok):
    bsz, _, n_tok = xs_t.shape
    n_blk = n_tok // CHUNK
    n_lat = n_lat_tok // CHUNK
    n_ctx = n_blk - n_lat

    def tb(d, j):
        fwd = jnp.where(j < n_ctx, n_lat + j, j - n_ctx)
        return jnp.where(d == 0, fwd, n_blk - 1 - j)

    t1 = lambda w: pl.BlockSpec((1, CHUNK, w), lambda d, b, j: (b, tb(d, j), 0))
    by_time = pl.BlockSpec((1, 1, CHUNK, 128), lambda d, b, j: (d, b, tb(d, j), 0))
    by_head = pl.BlockSpec((1, 1, 128, CHUNK), lambda d, b, j: (d, b, 0, tb(d, j)))
    return pl.pallas_call(
        _ssd_scan_kernel,
        grid=(2, bsz, n_blk),
        in_specs=[pl.BlockSpec((1, D_INNER, CHUNK), lambda d, b, j: (b, 0, tb(d, j))),
                  t1(SSD_GROUPS * D_STATE), t1(SSD_GROUPS * D_STATE), by_time, by_head, by_head],
        out_specs=pl.BlockSpec((1, 1, CHUNK, D_INNER), lambda d, b, j: (d, b, tb(d, j), 0)),
        out_shape=jax.ShapeDtypeStruct((2, bsz, n_tok, D_INNER), F32),
        scratch_shapes=[pltpu.VMEM((SSD_HEADS // 2, 2 * HEAD_DIM, D_STATE), F32)],
        compiler_params=_cparams(("parallel", "parallel", "arbitrary")),
        name="ssd_scan",
    )(xs_t, bm, cm, cs, cs_t, dtd_t)


def _ssd_out_kernel(x_ref, mod_ref, y_ref, xs_ref, z_ref, dsk_ref, ng_ref, wo_ref, out_ref):
    y = y_ref[0, 0] + y_ref[1, 0] + dsk_ref[...] * xs_ref[0]
    yg = y * _silu(z_ref[0])
    gw = D_INNER // SSD_GROUPS
    parts = []
    for g in range(SSD_GROUPS):
        t = yg[:, g * gw:(g + 1) * gw]
        ms = jnp.mean(t * t, axis=-1, keepdims=True)
        parts.append((t * lax.rsqrt(ms + NORM_EPS) * ng_ref[:, g * gw:(g + 1) * gw]).astype(BF16))
    yn = jnp.concatenate(parts, axis=1)
    out_ref[0] = x_ref[0] + mod_ref[0, 2:3, :] * jnp.dot(yn, wo_ref[...], preferred_element_type=F32)


def _ssd_out(x, mod, y, xs, z, d_skip, norm_g, w_out, n_lat_tiles):
    bsz = x.shape[0]
    t1 = lambda w: pl.BlockSpec((1, TM, w), lambda b, i: (b, i, 0))
    return pl.pallas_call(
        _ssd_out_kernel,
        grid=(bsz, n_lat_tiles),
        in_specs=[t1(D_MODEL), _mod_spec(n_lat_tiles),
                  pl.BlockSpec((2, 1, TM, D_INNER), lambda b, i: (0, b, i, 0)),
                  t1(D_INNER), t1(D_INNER), _const_spec(d_skip.shape), _const_spec(norm_g.shape),
                  _const_spec(w_out.shape)],
        out_specs=t1(D_MODEL),
        out_shape=jax.ShapeDtypeStruct((bsz, n_lat_tiles * TM, D_MODEL), F32),
        compiler_params=_cparams(("parallel", "parallel")),
        name="ssd_out",
    )(x, mod, y, xs, z, d_skip, norm_g, w_out)


def _rope_tables(t_len, n_ctx_tok):
    n_rows = t_len // GRID_W
    rows = jnp.broadcast_to(jnp.arange(n_rows)[:, None], (n_rows, GRID_W)).reshape(-1)
    cols = jnp.broadcast_to(jnp.arange(GRID_W)[None, :], (n_rows, GRID_W)).reshape(-1)
    inv = ROPE_BASE ** (-jnp.arange(0, HEAD_DIM // 2, 2, dtype=F32) / (HEAD_DIM // 2))
    ang_r = rows.astype(F32)[:, None] * inv
    ang_c = cols.astype(F32)[:, None] * inv
    cos = jnp.concatenate([jnp.cos(ang_r)] * 2 + [jnp.cos(ang_c)] * 2, axis=1)
    sin = jnp.concatenate([-jnp.sin(ang_r), jnp.sin(ang_r), -jnp.sin(ang_c), jnp.sin(ang_c)], axis=1)
    reps = DIFF_W // HEAD_DIM
    cos = jnp.concatenate([jnp.tile(cos, (1, reps)), jnp.ones((n_ctx_tok, DIFF_W), F32)], axis=0)
    sin = jnp.concatenate([jnp.tile(sin, (1, reps)), jnp.zeros((n_ctx_tok, DIFF_W), F32)], axis=0)
    return cos, sin


def kernel(x, c, ctx, c_ctx, ada_w, ada_b, norm1_g, norm2_g, mlp_w1, mlp_w2, hy_w_in, hy_w_out, rwkv_mu, rwkv_w0, rwkv_w_up, rwkv_a0, rwkv_a_up, rwkv_g_up, rwkv_k_k, rwkv_k_a, rwkv_r_k, rwkv_ln_w, rwkv_ln_b, diff_lq1, diff_lk1, diff_lq2, diff_lk2, diff_subln_g, ssd_w_in, ssd_conv_w, ssd_conv_b, ssd_dt_bias, ssd_a_log, ssd_d, ssd_norm_g, ssd_w_out, norm_f_g):
    bsz, t_len, _ = x.shape
    l_ctx = ctx.shape[1]
    assert l_ctx == TM and t_len % TM == 0 and t_len % l_ctx == 0
    depth = ada_w.shape[0]
    n_lat_tiles = t_len // TM
    row = lambda t: t.reshape(1, -1)

    cc = jnp.zeros((16, D_MODEL), F32).at[0:bsz].set(c).at[bsz].set(c_ctx)
    mods = _ada(cc, ada_w, ada_b)
    mod_lat = mods[:, 0:bsz]
    mod_ctx = jnp.broadcast_to(mods[:, bsz:bsz + 1], mod_lat.shape)
    mods = jnp.stack([mod_lat, mod_ctx], axis=2).reshape(depth, bsz * 2, 6, D_MODEL)

    xa = jnp.concatenate([x, ctx], axis=1)
    ones64 = (jnp.arange(RWKV_DIM)[:, None] // HEAD_DIM == jnp.arange(RWKV_DIM)[None, :] // HEAD_DIM).astype(BF16)
    cos_t, sin_t = _rope_tables(t_len, l_ctx)

    for li in range(depth):
        last = li == depth - 1
        mod = mods[li]
        if li % 2 == 0:
            e = li // 2
            lam_init = 0.8 - 0.6 * math.exp(-0.3 * li)
            lam = (jnp.exp(jnp.sum(diff_lq1[e] * diff_lk1[e])) - jnp.exp(jnp.sum(diff_lq2[e] * diff_lk2[e]))
                   + lam_init).reshape(1).astype(F32)
            zero = jnp.zeros((64, RWKV_DIM), F32)
            lora = jnp.stack([jnp.concatenate(
                [jnp.concatenate([rwkv_w_up[e, d], zero], axis=1),
                 jnp.concatenate([zero, rwkv_a_up[e, d]], axis=1)], axis=0) for d in range(2)]).astype(BF16)
            (r, v, a, dec, km, bb, g, qr, qp, ks, vd) = _hyb_in(
                xa, row(norm1_g[li]), mod, hy_w_in[e].astype(BF16), row(rwkv_mu[e]), row(rwkv_k_k[e]),
                row(rwkv_k_a[e]), rwkv_w0[e], rwkv_a0[e], lora, rwkv_g_up[e].astype(BF16), ones64,
                cos_t, sin_t, n_lat_tiles)
            yf = _wkv(r, v, a, dec, km, bb, t_len, False)
            yb = _wkv(r, v, a, dec, km, bb, t_len, True)
            o_lat = _attn(lam, qr, qp, ks, vd, t_len, True)
            o_ctx = _attn(lam, qr, qp, ks, vd, t_len, False)
            xa = _hyb_out(xa, mod, yf, yb, r, v, km, g, o_lat, o_ctx, ones64, row(rwkv_r_k[e]),
                          row(rwkv_ln_w[e]), row(rwkv_ln_b[e]),
                          row(diff_subln_g[e]), hy_w_out[e].astype(BF16), n_lat_tiles, 1.0 - lam_init)
        else:
            o = li // 2
            w_in = ssd_w_in[o]
            w_zx = w_in[:, 0:D_INNER + CONV_DIM].astype(BF16)
            w_dt = jnp.pad(w_in[:, D_INNER + CONV_DIM:], ((0, 0), (0, 128 - SSD_HEADS))).astype(BF16)
            pad_h = lambda t: jnp.pad(t, ((0, 0), (0, 128 - SSD_HEADS))).reshape(2, 1, 128)
            z, xs, xs_t, bm, cm, cs, cs_t, dtd_t = _ssd_in(
                xa, row(norm1_g[li]), mod, w_zx, w_dt, ssd_conv_w[o], row(ssd_conv_b[o]),
                pad_h(ssd_dt_bias[o]), pad_h(ssd_a_log[o]), n_lat_tiles)
            y = _ssd_scan(xs_t, bm, cm, cs, cs_t, dtd_t, t_len)
            d_skip = row(jnp.repeat(ssd_d[o], HEAD_DIM))
            if last:
                xa = _ssd_out(xa, mod, y, xs, z, d_skip, row(ssd_norm_g[o]), ssd_w_out[o].astype(BF16),
                              n_lat_tiles)
            else:
                raise NotImplementedError("an SSD layer that is not the last layer")
        xa = _mlp(xa, mod, row(norm2_g[li]), mlp_w1[li].astype(BF16), mlp_w2[li].astype(BF16),
                  row(norm_f_g), n_lat_tiles, last)
    return xa[:, 0:t_len]
```
